```python
import math
import jax, jax.numpy as jnp
from jax import lax
import numpy as np

D_MODEL = 1024
BATCH = 8
SEQ = 8192
DEPTH = 2

MIX_WIDTH = D_MODEL
CONV_CH = 512
CONV_WIDTH = 31
RET_HEADS = 4
RET_HEAD_DIM = 128
RET_WIDTH = RET_HEADS * RET_HEAD_DIM
RET_CHUNK = 128
ROPE_BASE = 10000.0
D_FF = 4 * D_MODEL
PLE_DIM = 256
IN_COLS = 2 * CONV_CH + 4 * RET_WIDTH
RMS_EPS = 1e-6
LN_EPS = 1e-5

kernel_name = "hybrid_conformer_conv_retention_block"


def rms_norm(x, g):
    xf = x.astype(jnp.float32)
    y = xf * lax.rsqrt(jnp.mean(xf * xf, axis=-1, keepdims=True) + RMS_EPS)
    return (y * g.astype(jnp.float32)).astype(x.dtype)


def layer_norm_f32(x, g, b):
    xf = x.astype(jnp.float32)
    mu = jnp.mean(xf, axis=-1, keepdims=True)
    var = jnp.mean(jnp.square(xf - mu), axis=-1, keepdims=True)
    return (xf - mu) * lax.rsqrt(var + LN_EPS) * g.astype(jnp.float32) + b.astype(jnp.float32)


def rotary(t, positions):
    half = t.shape[-1] // 2
    inv_freq = 1.0 / (ROPE_BASE ** (jnp.arange(0, half, dtype=jnp.float32) * (2.0 / t.shape[-1])))
    ang = positions.astype(jnp.float32)[..., None] * inv_freq
    cos = jnp.cos(ang)[:, :, None, :]
    sin = jnp.sin(ang)[:, :, None, :]
    t1, t2 = t[..., :half], t[..., half:]
    return jnp.concatenate([t1 * cos - t2 * sin, t2 * cos + t1 * sin], axis=-1)


def conv_module(u, dw_w, dw_b, ln_g, ln_b):
    a, b = jnp.split(u, 2, axis=-1)
    h = a * jax.nn.sigmoid(b)
    h = lax.conv_general_dilated(
        h, dw_w[:, None, :].astype(h.dtype), window_strides=(1,),
        padding=[(CONV_WIDTH - 1, 0)],
        dimension_numbers=('NWC', 'WIO', 'NWC'),
        feature_group_count=CONV_CH) + dw_b
    h = layer_norm_f32(h, ln_g, ln_b)
    return jax.nn.silu(h).astype(u.dtype)


def chunkwise_retention(q, k, v, positions):
    B, S = q.shape[0], q.shape[1]
    n_chunks = S // RET_CHUNK
    shp = (B, S, RET_HEADS, RET_HEAD_DIM)
    qf = rotary(q.astype(jnp.float32).reshape(shp), positions)
    kf = rotary(k.astype(jnp.float32).reshape(shp), positions) * (RET_HEAD_DIM ** -0.5)
    vf = v.astype(jnp.float32).reshape(shp)

    def to_chunks(t):
        return t.reshape(B, n_chunks, RET_CHUNK, RET_HEADS, RET_HEAD_DIM).transpose(1, 0, 3, 2, 4)

    qc, kc, vc = to_chunks(qf), to_chunks(kf), to_chunks(vf)

    log_g = jnp.log(1.0 - jnp.exp2(-5.0 - jnp.arange(RET_HEADS, dtype=jnp.float32)))
    idx = jnp.arange(RET_CHUNK, dtype=jnp.float32)
    diff = idx[:, None] - idx[None, :]
    decay_mask = jnp.where(diff[None] >= 0,
                           jnp.exp(jnp.maximum(diff, 0.0)[None] * log_g[:, None, None]), 0.0)
    zeta = jnp.exp((RET_CHUNK - 1 - idx)[None, :] * log_g[:, None])
    xi = jnp.exp((idx + 1.0)[None, :] * log_g[:, None])
    chunk_decay = jnp.exp(RET_CHUNK * log_g)

    def step(state, inp):
        qi, ki, vi = inp
        scores = jnp.einsum('bhqd,bhkd->bhqk', qi, ki) * decay_mask
        inner = jnp.einsum('bhqk,bhkd->bhqd', scores, vi)
        cross = jnp.einsum('bhqd,bhde->bhqe', qi, state) * xi[None, :, :, None]
        new_state = state * chunk_decay[None, :, None, None] + jnp.einsum(
            'bhkd,bhke->bhde', ki * zeta[None, :, :, None], vi)
        return new_state, inner + cross

    state0 = jnp.zeros((B, RET_HEADS, RET_HEAD_DIM, RET_HEAD_DIM), jnp.float32)
    _, out = lax.scan(step, state0, (qc, kc, vc))
    return out.transpose(1, 0, 3, 2, 4).reshape(shp)


def setup_inputs(seed: int = 0) -> dict:
    key = jax.random.key(seed)
    ks = jax.random.split(key, 20)
    f32 = jnp.float32

    def nrm(k, shape, scale):
        return jax.random.normal(k, shape, f32) * scale

    def gain(k, shape):
        return 1.0 + 0.02 * jax.random.normal(k, shape, f32)

    x = jax.random.normal(ks[0], (BATCH, SEQ, D_MODEL), f32)
    p = jax.random.normal(ks[1], (DEPTH, BATCH, SEQ, PLE_DIM), f32)
    positions = jnp.broadcast_to(jnp.arange(SEQ, dtype=jnp.int32)[None, :], (BATCH, SEQ))
    return {
        "x": x,
        "p": p,
        "positions": positions,
        "norm_mix_g": gain(ks[2], (DEPTH, D_MODEL)),
        "w_in": nrm(ks[3], (DEPTH, D_MODEL, IN_COLS), D_MODEL ** -0.5),
        "conv_dw_w": nrm(ks[4], (DEPTH, CONV_WIDTH, CONV_CH), CONV_WIDTH ** -0.5),
        "conv_dw_b": nrm(ks[5], (DEPTH, CONV_CH), 0.02),
        "conv_ln_g": gain(ks[6], (DEPTH, CONV_CH)),
        "conv_ln_b": nrm(ks[7], (DEPTH, CONV_CH), 0.02),
        "ret_gn_g": gain(ks[8], (DEPTH, RET_WIDTH)),
        "ret_gn_b": nrm(ks[9], (DEPTH, RET_WIDTH), 0.02),
        "w_out": nrm(ks[10], (DEPTH, MIX_WIDTH, D_MODEL), MIX_WIDTH ** -0.5),
        "norm_ffn_g": gain(ks[11], (DEPTH, D_MODEL)),
        "w_ff1": nrm(ks[12], (DEPTH, D_MODEL, D_FF), D_MODEL ** -0.5),
        "w_ff2": nrm(ks[13], (DEPTH, D_FF, D_MODEL), D_FF ** -0.5),
        "norm_ple_g": gain(ks[14], (DEPTH, D_MODEL)),
        "w_ple_gate": nrm(ks[15], (DEPTH, D_MODEL, D_MODEL), D_MODEL ** -0.5),
        "w_ple_proj": nrm(ks[16], (DEPTH, PLE_DIM, D_MODEL), PLE_DIM ** -0.5),
        "final_norm_g": gain(ks[17], (D_MODEL,)),
    }


def reference(x, p, positions, norm_mix_g, w_in, conv_dw_w, conv_dw_b, conv_ln_g, conv_ln_b,
              ret_gn_g, ret_gn_b, w_out, norm_ffn_g, w_ff1, w_ff2, norm_ple_g,
              w_ple_gate, w_ple_proj, final_norm_g):
    B, S = x.shape[0], x.shape[1]
    splits = [2 * CONV_CH, 2 * CONV_CH + RET_WIDTH, 2 * CONV_CH + 2 * RET_WIDTH,
              2 * CONV_CH + 3 * RET_WIDTH]
    for i in range(DEPTH):
        h = rms_norm(x, norm_mix_g[i])
        z = h @ w_in[i]
        u_conv, q, k, v, g = jnp.split(z, splits, axis=-1)
        conv_out = conv_module(u_conv, conv_dw_w[i], conv_dw_b[i], conv_ln_g[i], conv_ln_b[i])
        ret = chunkwise_retention(q, k, v, positions)
        mu = jnp.mean(ret, axis=-1, keepdims=True)
        var = jnp.mean(jnp.square(ret - mu), axis=-1, keepdims=True)
        ret = ((ret - mu) * lax.rsqrt(var + LN_EPS)).reshape(B, S, RET_WIDTH)
        ret = ret * ret_gn_g[i].astype(jnp.float32) + ret_gn_b[i].astype(jnp.float32)
        ret_out = (jax.nn.silu(g.astype(jnp.float32)) * ret).astype(x.dtype)
        mix = jnp.concatenate([conv_out, ret_out], axis=-1) @ w_out[i]
        x = x + mix
        h = rms_norm(x, norm_ffn_g[i])
        x = x + jnp.square(jax.nn.relu(h @ w_ff1[i])) @ w_ff2[i]
        h = rms_norm(x, norm_ple_g[i])
        x = x + jax.nn.sigmoid(h @ w_ple_gate[i]) * (p[i] @ w_ple_proj[i])
    return rms_norm(x, final_norm_g)
```

```python
import functools

import jax
import jax.numpy as jnp
from jax import lax
from jax.experimental import pallas as pl
from jax.experimental.pallas import tpu as pltpu

D_MODEL = 1024
CONV_CH = 512
CONV_WIDTH = 31
RET_HEADS = 4
HEAD_DIM = 128
RET_WIDTH = RET_HEADS * HEAD_DIM
RET_CHUNK = 128
ROPE_BASE = 10000.0
D_FF = 4 * D_MODEL
PLE_DIM = 256
RMS_EPS = 1e-6
LN_EPS = 1e-5

LANES = 128
SUBLANES = 8
VMEM_LIMIT_BYTES = 56 * 1024 * 1024

HALO_ROWS = 32
CONV_ROW_BLOCK = 64
TM_MIX_IN = 512
TS_MIX_OUT = 512
TM_FFN = 512
FF_CHUNK = 1024
TM_ROPE = 2048

F32 = jnp.float32
BF16 = jnp.bfloat16


def _dot(a, b):
    return jnp.dot(a, b, preferred_element_type=F32)


def _rms_norm(x, g):
    ms = jnp.mean(x * x, axis=-1, keepdims=True)
    return x * lax.rsqrt(ms + RMS_EPS) * g


def _sigmoid(x):
    return 1.0 / (1.0 + jnp.exp(-x))


def _rope_kernel(pos_ref, invf_ref, sign_ref, cos_ref, sin_ref):
    ang = pos_ref[...].astype(F32) * invf_ref[...]
    cos_ref[...] = jnp.cos(ang)
    sin_ref[...] = jnp.sin(ang) * sign_ref[...]


def _rope_tables(positions):
    m = positions.size
    half = HEAD_DIM // 2
    inv_freq = 1.0 / (ROPE_BASE ** (jnp.arange(0, half, dtype=F32) * (2.0 / HEAD_DIM)))
    invf = jnp.concatenate([inv_freq, inv_freq])[None, :]
    sign = jnp.concatenate([-jnp.ones((half,), F32), jnp.ones((half,), F32)])[None, :]
    pos = positions.reshape(m, 1)
    row = lambda i: (i, 0)
    const = lambda i: (0, 0)
    return pl.pallas_call(
        _rope_kernel,
        grid=(m // TM_ROPE,),
        in_specs=[pl.BlockSpec((TM_ROPE, 1), row),
                  pl.BlockSpec((1, LANES), const),
                  pl.BlockSpec((1, LANES), const)],
        out_specs=[pl.BlockSpec((TM_ROPE, LANES), row),
                   pl.BlockSpec((TM_ROPE, LANES), row)],
        out_shape=[jax.ShapeDtypeStruct((m, LANES), F32)] * 2,
        compiler_params=pltpu.CompilerParams(dimension_semantics=("arbitrary",)),
        name="rope_tables",
    )(pos, invf, sign)


def _mix_in_kernel(x_ref, cos_ref, sin_ref, g_ref, w_ref, cw_ref, cb_ref, lng_ref, lnb_ref,
                   conv_ref, q_ref, k_ref, v_ref, gate_ref, hbuf):
    tm = x_ref.shape[0]

    @pl.when(pl.program_id(1) == 0)
    def _():
        hbuf[0:HALO_ROWS, :] = jnp.zeros((HALO_ROWS, CONV_CH), F32)

    h = _rms_norm(x_ref[...], g_ref[...]).astype(BF16)

    a = _dot(h, w_ref[:, 0:CONV_CH])
    b = _dot(h, w_ref[:, CONV_CH:2 * CONV_CH])
    hbuf[HALO_ROWS:HALO_ROWS + tm, :] = a * _sigmoid(b)

    col = 2 * CONV_CH
    cos = cos_ref[...]
    sin = sin_ref[...]
    for dst, scale in ((q_ref, None), (k_ref, HEAD_DIM ** -0.5)):
        z = _dot(h, w_ref[:, col:col + RET_WIDTH])
        for hd in range(RET_HEADS):
            zh = z[:, hd * HEAD_DIM:(hd + 1) * HEAD_DIM]
            zr = zh * cos + pltpu.roll(zh, HEAD_DIM // 2, 1) * sin
            if scale is not None:
                zr = zr * scale
            dst[:, hd * HEAD_DIM:(hd + 1) * HEAD_DIM] = zr.astype(BF16)
        col += RET_WIDTH
    v_ref[...] = _dot(h, w_ref[:, col:col + RET_WIDTH]).astype(BF16)
    col += RET_WIDTH
    gate_ref[...] = _dot(h, w_ref[:, col:col + RET_WIDTH]).astype(BF16)

    n_cb = CONV_CH // LANES
    first = HALO_ROWS - (CONV_WIDTH - 1)
    for rb in range(tm // CONV_ROW_BLOCK):
        r0 = rb * CONV_ROW_BLOCK
        accs = []
        for cb in range(n_cb):
            c0 = cb * LANES
            acc = None
            for t in range(CONV_WIDTH):
                seg = hbuf[first + r0 + t:first + r0 + t + CONV_ROW_BLOCK, c0:c0 + LANES]
                term = seg * cw_ref[t:t + 1, c0:c0 + LANES]
                acc = term if acc is None else acc + term
            accs.append(acc + cb_ref[:, c0:c0 + LANES])
        tot = accs[0]
        for cb in range(1, n_cb):
            tot = tot + accs[cb]
        mu = jnp.sum(tot, axis=-1, keepdims=True) * (1.0 / CONV_CH)
        devs = [acc - mu for acc in accs]
        sq = devs[0] * devs[0]
        for cb in range(1, n_cb):
            sq = sq + devs[cb] * devs[cb]
        var = jnp.sum(sq, axis=-1, keepdims=True) * (1.0 / CONV_CH)
        inv = lax.rsqrt(var + LN_EPS)
        for cb in range(n_cb):
            c0 = cb * LANES
            y = devs[cb] * inv * lng_ref[:, c0:c0 + LANES] + lnb_ref[:, c0:c0 + LANES]
            conv_ref[r0:r0 + CONV_ROW_BLOCK, c0:c0 + LANES] = (y * _sigmoid(y)).astype(BF16)

    hbuf[0:HALO_ROWS, :] = hbuf[tm:tm + HALO_ROWS, :]


def _mix_in(x, cos_t, sin_t, norm_g, w_in, cw, cb, lng, lnb):
    bsz, seq, _ = x.shape
    tm = TM_MIX_IN
    tile = lambda b, s: (b, s, 0)
    const = lambda b, s: (0, 0)
    in_cols = w_in.shape[1]
    out_sds = jax.ShapeDtypeStruct((bsz, seq, RET_WIDTH), BF16)
    return pl.pallas_call(
        _mix_in_kernel,
        grid=(bsz, seq // tm),
        in_specs=[pl.BlockSpec((None, tm, D_MODEL), tile),
                  pl.BlockSpec((None, tm, LANES), tile),
                  pl.BlockSpec((None, tm, LANES), tile),
                  pl.BlockSpec((1, D_MODEL), const),
                  pl.BlockSpec((D_MODEL, in_cols), const),
                  pl.BlockSpec((CONV_WIDTH, CONV_CH), const),
                  pl.BlockSpec((1, CONV_CH), const),
                  pl.BlockSpec((1, CONV_CH), const),
                  pl.BlockSpec((1, CONV_CH), const)],
        out_specs=[pl.BlockSpec((None, tm, RET_WIDTH), tile)] * 5,
        out_shape=[out_sds] * 5,
        scratch_shapes=[pltpu.VMEM((HALO_ROWS + tm, CONV_CH), F32)],
        compiler_params=pltpu.CompilerParams(
            dimension_semantics=("arbitrary", "arbitrary"),
            vmem_limit_bytes=VMEM_LIMIT_BYTES),
        name="mix_in",
    )(x, cos_t, sin_t, norm_g, w_in, cw, cb, lng, lnb)


def _mix_out_kernel(x_ref, conv_ref, q_ref, k_ref, v_ref, gate_ref, w_ref,
                    mask_ref, xi_ref, zeta_ref, cd_ref, gng_ref, gnb_ref,
                    o_ref, state, ret_buf):
    ts = x_ref.shape[0]

    @pl.when(pl.program_id(1) == 0)
    def _():
        state[...] = jnp.zeros(state.shape, F32)

    for c in range(ts // RET_CHUNK):
        r0 = c * RET_CHUNK
        for hd in range(RET_HEADS):
            c0 = hd * HEAD_DIM
            qh = q_ref[r0:r0 + RET_CHUNK, c0:c0 + HEAD_DIM]
            kh = k_ref[r0:r0 + RET_CHUNK, c0:c0 + HEAD_DIM]
            vh = v_ref[r0:r0 + RET_CHUNK, c0:c0 + HEAD_DIM]
            st = state[hd]
            scores = lax.dot_general(qh, kh, (((1,), (1,)), ((), ())),
                                     preferred_element_type=F32) * mask_ref[hd]
            qx = (qh.astype(F32) * xi_ref[hd]).astype(BF16)
            lhs = jnp.concatenate([scores.astype(BF16), qx], axis=1)
            rhs = jnp.concatenate([vh, st.astype(BF16)], axis=0)
            o = _dot(lhs, rhs)
            kz = (kh.astype(F32) * zeta_ref[hd]).astype(BF16)
            kv = lax.dot_general(kz, vh, (((0,), (0,)), ((), ())),
                                 preferred_element_type=F32)
            state[hd] = st * cd_ref[hd] + kv
            mu = jnp.mean(o, axis=-1, keepdims=True)
            d = o - mu
            var = jnp.mean(d * d, axis=-1, keepdims=True)
            y = d * lax.rsqrt(var + LN_EPS) * gng_ref[:, c0:c0 + HEAD_DIM] + gnb_ref[:, c0:c0 + HEAD_DIM]
            g = gate_ref[r0:r0 + RET_CHUNK, c0:c0 + HEAD_DIM].astype(F32)
            ret_buf[r0:r0 + RET_CHUNK, c0:c0 + HEAD_DIM] = (g * _sigmoid(g) * y).astype(BF16)

    mix = _dot(conv_ref[...], w_ref[0:CONV_CH, :]) + _dot(ret_buf[...], w_ref[CONV_CH:, :])
    o_ref[...] = x_ref[...] + mix


def _retention_constants():
    log_g = jnp.log(1.0 - jnp.exp2(-5.0 - jnp.arange(RET_HEADS, dtype=F32)))
    idx = jnp.arange(RET_CHUNK, dtype=F32)
    diff = idx[:, None] - idx[None, :]
    mask = jnp.where(diff[None] >= 0,
                     jnp.exp(jnp.maximum(diff, 0.0)[None] * log_g[:, None, None]), 0.0)
    zeta = jnp.exp((RET_CHUNK - 1 - idx)[None, :] * log_g[:, None])
    xi = jnp.exp((idx + 1.0)[None, :] * log_g[:, None])
    cd = jnp.exp(RET_CHUNK * log_g)
    full = (RET_HEADS, RET_CHUNK, HEAD_DIM)
    return (mask,
            jnp.broadcast_to(xi[:, :, None], full),
            jnp.broadcast_to(zeta[:, :, None], full),
            jnp.broadcast_to(cd[:, None, None], full))


def _mix_out(x, conv, q, k, v, gate, w_out, consts, gng, gnb):
    bsz, seq, _ = x.shape
    ts = TS_MIX_OUT
    tile = lambda b, s: (b, s, 0)
    const2 = lambda b, s: (0, 0)
    const3 = lambda b, s: (0, 0, 0)
    head_const = pl.BlockSpec((RET_HEADS, RET_CHUNK, HEAD_DIM), const3)
    branch = pl.BlockSpec((None, ts, RET_WIDTH), tile)
    return pl.pallas_call(
        _mix_out_kernel,
        grid=(bsz, seq // ts),
        in_specs=[pl.BlockSpec((None, ts, D_MODEL), tile),
                  branch, branch, branch, branch, branch,
                  pl.BlockSpec((D_MODEL, D_MODEL), const2),
                  head_const, head_const, head_const, head_const,
                  pl.BlockSpec((1, RET_WIDTH), const2),
                  pl.BlockSpec((1, RET_WIDTH), const2)],
        out_specs=pl.BlockSpec((None, ts, D_MODEL), tile),
        out_shape=jax.ShapeDtypeStruct(x.shape, F32),
        scratch_shapes=[pltpu.VMEM((RET_HEADS, HEAD_DIM, HEAD_DIM), F32),
                        pltpu.VMEM((ts, RET_WIDTH), BF16)],
        compiler_params=pltpu.CompilerParams(
            dimension_semantics=("arbitrary", "arbitrary"),
            vmem_limit_bytes=VMEM_LIMIT_BYTES),
        name="mix_out",
    )(x, conv, q, k, v, gate, w_out, *consts, gng, gnb)


def _ffn_ple_kernel(x_ref, p_ref, gf_ref, w1_ref, w2_ref, gp_ref, wg_ref, wp_ref, gl_ref,
                    o_ref, *, final_norm):
    x = x_ref[...]
    h = _rms_norm(x, gf_ref[...]).astype(BF16)
    acc = None
    for j in range(D_FF // FF_CHUNK):
        c0 = j * FF_CHUNK
        a = jnp.maximum(_dot(h, w1_ref[:, c0:c0 + FF_CHUNK]), 0.0)
        part = _dot((a * a).astype(BF16), w2_ref[c0:c0 + FF_CHUNK, :])
        acc = part if acc is None else acc + part
    x = x + acc
    h = _rms_norm(x, gp_ref[...]).astype(BF16)
    gate = _sigmoid(_dot(h, wg_ref[...]))
    x = x + gate * _dot(p_ref[...].astype(BF16), wp_ref[...])
    if final_norm:
        x = _rms_norm(x, gl_ref[...])
    o_ref[...] = x


def _ffn_ple(x, p, gf, w1, w2, gp, wg, wp, gl, final_norm):
    m = x.shape[0]
    tm = TM_FFN
    row = lambda i: (i, 0)
    const = lambda i: (0, 0)
    resident = lambda shape: pl.BlockSpec(shape, const, pipeline_mode=pl.Buffered(1))
    return pl.pallas_call(
        functools.partial(_ffn_ple_kernel, final_norm=final_norm),
        grid=(m // tm,),
        in_specs=[pl.BlockSpec((tm, D_MODEL), row),
                  pl.BlockSpec((tm, PLE_DIM), row),
                  pl.BlockSpec((1, D_MODEL), const),
                  resident((D_MODEL, D_FF)),
                  resident((D_FF, D_MODEL)),
                  pl.BlockSpec((1, D_MODEL), const),
                  resident((D_MODEL, D_MODEL)),
                  resident((PLE_DIM, D_MODEL)),
                  pl.BlockSpec((1, D_MODEL), const)],
        out_specs=pl.BlockSpec((tm, D_MODEL), row),
        out_shape=jax.ShapeDtypeStruct(x.shape, F32),
        compiler_params=pltpu.CompilerParams(
            dimension_semantics=("arbitrary",),
            vmem_limit_bytes=VMEM_LIMIT_BYTES),
        name="ffn_ple_final" if final_norm else "ffn_ple",
    )(x, p, gf, w1, w2, gp, wg, wp, gl)


def kernel(x, p, positions, norm_mix_g, w_in, conv_dw_w, conv_dw_b, conv_ln_g, conv_ln_b,
           ret_gn_g, ret_gn_b, w_out, norm_ffn_g, w_ff1, w_ff2, norm_ple_g,
           w_ple_gate, w_ple_proj, final_norm_g):
    bsz, seq, d = x.shape
    depth = w_in.shape[0]
    assert d == D_MODEL and seq % TM_MIX_IN == 0 and seq % TS_MIX_OUT == 0
    assert (bsz * seq) % TM_FFN == 0 and (bsz * seq) % TM_ROPE == 0

    cos_t, sin_t = _rope_tables(positions)
    cos_t = cos_t.reshape(bsz, seq, LANES)
    sin_t = sin_t.reshape(bsz, seq, LANES)
    consts = _retention_constants()
    row = lambda v: v[None, :]

    for i in range(depth):
        conv, q, k, v, gate = _mix_in(
            x, cos_t, sin_t, row(norm_mix_g[i]), w_in[i].astype(BF16),
            conv_dw_w[i], row(conv_dw_b[i]), row(conv_ln_g[i]), row(conv_ln_b[i]))
        x = _mix_out(x, conv, q, k, v, gate, w_out[i].astype(BF16), consts,
                     row(ret_gn_g[i]), row(ret_gn_b[i]))
        x = _ffn_ple(
            x.reshape(bsz * seq, d), p[i].reshape(bsz * seq, PLE_DIM),
            row(norm_ffn_g[i]), w_ff1[i].astype(BF16), w_ff2[i].astype(BF16),
            row(norm_ple_g[i]), w_ple_gate[i].astype(BF16), w_ple_proj[i].astype(BF16),
            row(final_norm_g), final_norm=(i == depth - 1)).reshape(bsz, seq, d)
    return x
```

```python
import functools

import jax
import jax.numpy as jnp
from jax import lax
from jax.experimental import pallas as pl
from jax.experimental.pallas import tpu as pltpu

D_MODEL = 1024
CONV_CH = 512
CONV_WIDTH = 31
RET_HEADS = 4
HEAD_DIM = 128
RET_WIDTH = RET_HEADS * HEAD_DIM
RET_CHUNK = 128
ROPE_BASE = 10000.0
D_FF = 4 * D_MODEL
PLE_DIM = 256
RMS_EPS = 1e-6
LN_EPS = 1e-5

LANES = 128
SUBLANES = 8
VMEM_LIMIT_BYTES = 56 * 1024 * 1024

HALO_ROWS = 32
CONV_ROW_BLOCK = 64
TM_MIX_IN = 512
TS_MIX_OUT = 512
TM_FFN = 512
FF_CHUNK = 1024
TM_ROPE = 2048

F32 = jnp.float32
BF16 = jnp.bfloat16


def _dot(a, b):
    return jnp.dot(a, b, preferred_element_type=F32)


def _rms_norm(x, g):
    ms = jnp.mean(x * x, axis=-1, keepdims=True)
    return x * lax.rsqrt(ms + RMS_EPS) * g


def _sigmoid(x):
    return 1.0 / (1.0 + jnp.exp(-x))


def _rope_kernel(pos_ref, invf_ref, sign_ref, cos_ref, sin_ref):
    ang = pos_ref[...].astype(F32) * invf_ref[...]
    cos_ref[...] = jnp.cos(ang)
    sin_ref[...] = jnp.sin(ang) * sign_ref[...]


def _rope_tables(positions):
    m = positions.size
    half = HEAD_DIM // 2
    inv_freq = 1.0 / (ROPE_BASE ** (jnp.arange(0, half, dtype=F32) * (2.0 / HEAD_DIM)))
    invf = jnp.concatenate([inv_freq, inv_freq])[None, :]
    sign = jnp.concatenate([-jnp.ones((half,), F32), jnp.ones((half,), F32)])[None, :]
    pos = positions.reshape(m, 1)
    row = lambda i: (i, 0)
    const = lambda i: (0, 0)
    return pl.pallas_call(
        _rope_kernel,
        grid=(m // TM_ROPE,),
        in_specs=[pl.BlockSpec((TM_ROPE, 1), row),
                  pl.BlockSpec((1, LANES), const),
                  pl.BlockSpec((1, LANES), const)],
        out_specs=[pl.BlockSpec((TM_ROPE, LANES), row),
                   pl.BlockSpec((TM_ROPE, LANES), row)],
        out_shape=[jax.ShapeDtypeStruct((m, LANES), F32)] * 2,
        compiler_params=pltpu.CompilerParams(dimension_semantics=("arbitrary",)),
        name="rope_tables",
    )(pos, invf, sign)


def _mix_in_kernel(x_ref, cos_ref, sin_ref, g_ref, w_ref, cw_ref, cb_ref, lng_ref, lnb_ref,
                   conv_ref, q_ref, k_ref, v_ref, gate_ref, hbuf):
    tm = x_ref.shape[0]
    n_cb = CONV_CH // LANES

    @pl.when(pl.program_id(1) == 0)
    def _():
        hbuf[:, 0:2 * HALO_ROWS, :] = jnp.zeros((n_cb, 2 * HALO_ROWS, LANES), F32)

    h = _rms_norm(x_ref[...], g_ref[...]).astype(BF16)

    a = _dot(h, w_ref[:, 0:CONV_CH])
    b = _dot(h, w_ref[:, CONV_CH:2 * CONV_CH])
    glu = a * _sigmoid(b)
    for cb in range(n_cb):
        hbuf[cb, pl.ds(2 * HALO_ROWS, tm, stride=2), :] = glu[:, cb * LANES:(cb + 1) * LANES]

    col = 2 * CONV_CH
    cos = cos_ref[...]
    sin = sin_ref[...]
    for dst, scale in ((q_ref, None), (k_ref, HEAD_DIM ** -0.5)):
        z = _dot(h, w_ref[:, col:col + RET_WIDTH])
        for hd in range(RET_HEADS):
            zh = z[:, hd * HEAD_DIM:(hd + 1) * HEAD_DIM]
            zr = zh * cos + pltpu.roll(zh, HEAD_DIM // 2, 1) * sin
            if scale is not None:
                zr = zr * scale
            dst[:, hd * HEAD_DIM:(hd + 1) * HEAD_DIM] = zr.astype(BF16)
        col += RET_WIDTH
    v_ref[...] = _dot(h, w_ref[:, col:col + RET_WIDTH]).astype(BF16)
    col += RET_WIDTH
    gate_ref[...] = _dot(h, w_ref[:, col:col + RET_WIDTH]).astype(BF16)

    first = HALO_ROWS - (CONV_WIDTH - 1)
    for rb in range(tm // CONV_ROW_BLOCK):
        r0 = rb * CONV_ROW_BLOCK
        accs = []
        for cb in range(n_cb):
            c0 = cb * LANES
            acc = None
            for t in range(CONV_WIDTH):
                seg = hbuf[cb, pl.ds(2 * (first + r0 + t), CONV_ROW_BLOCK, stride=2), :]
                term = seg * cw_ref[t:t + 1, c0:c0 + LANES]
                acc = term if acc is None else acc + term
            accs.append(acc + cb_ref[:, c0:c0 + LANES])
        tot = accs[0]
        for cb in range(1, n_cb):
            tot = tot + accs[cb]
        mu = jnp.sum(tot, axis=-1, keepdims=True) * (1.0 / CONV_CH)
        devs = [acc - mu for acc in accs]
        sq = devs[0] * devs[0]
        for cb in range(1, n_cb):
            sq = sq + devs[cb] * devs[cb]
        var = jnp.sum(sq, axis=-1, keepdims=True) * (1.0 / CONV_CH)
        inv = lax.rsqrt(var + LN_EPS)
        for cb in range(n_cb):
            c0 = cb * LANES
            y = devs[cb] * inv * lng_ref[:, c0:c0 + LANES] + lnb_ref[:, c0:c0 + LANES]
            conv_ref[r0:r0 + CONV_ROW_BLOCK, c0:c0 + LANES] = (y * _sigmoid(y)).astype(BF16)

    for cb in range(n_cb):
        hbuf[cb, pl.ds(0, HALO_ROWS, stride=2), :] = hbuf[cb, pl.ds(2 * tm, HALO_ROWS, stride=2), :]


def _mix_in(x, cos_t, sin_t, norm_g, w_in, cw, cb, lng, lnb):
    bsz, seq, _ = x.shape
    tm = TM_MIX_IN
    tile = lambda b, s: (b, s, 0)
    const = lambda b, s: (0, 0)
    in_cols = w_in.shape[1]
    out_sds = jax.ShapeDtypeStruct((bsz, seq, RET_WIDTH), BF16)
    return pl.pallas_call(
        _mix_in_kernel,
        grid=(bsz, seq // tm),
        in_specs=[pl.BlockSpec((None, tm, D_MODEL), tile),
                  pl.BlockSpec((None, tm, LANES), tile),
                  pl.BlockSpec((None, tm, LANES), tile),
                  pl.BlockSpec((1, D_MODEL), const),
                  pl.BlockSpec((D_MODEL, in_cols), const),
                  pl.BlockSpec((CONV_WIDTH, CONV_CH), const),
                  pl.BlockSpec((1, CONV_CH), const),
                  pl.BlockSpec((1, CONV_CH), const),
                  pl.BlockSpec((1, CONV_CH), const)],
        out_specs=[pl.BlockSpec((None, tm, RET_WIDTH), tile)] * 5,
        out_shape=[out_sds] * 5,
        scratch_shapes=[pltpu.VMEM((CONV_CH // LANES, 2 * (HALO_ROWS + tm), LANES), F32)],
        compiler_params=pltpu.CompilerParams(
            dimension_semantics=("arbitrary", "arbitrary"),
            vmem_limit_bytes=VMEM_LIMIT_BYTES),
        name="mix_in",
    )(x, cos_t, sin_t, norm_g, w_in, cw, cb, lng, lnb)


def _mix_out_kernel(x_ref, conv_ref, q_ref, k_ref, v_ref, gate_ref, w_ref,
                    mask_ref, xi_ref, zeta_ref, cd_ref, gng_ref, gnb_ref,
                    o_ref, state, ret_buf):
    ts = x_ref.shape[0]

    @pl.when(pl.program_id(1) == 0)
    def _():
        state[...] = jnp.zeros(state.shape, F32)

    n_chunks = ts // RET_CHUNK
    blocks = [(c, hd) for c in range(n_chunks) for hd in range(RET_HEADS)]

    def head_block(ref, c, hd):
        return ref[c * RET_CHUNK:(c + 1) * RET_CHUNK, hd * HEAD_DIM:(hd + 1) * HEAD_DIM]

    scores, kvs = {}, {}
    for c, hd in blocks:
        qh, kh, vh = head_block(q_ref, c, hd), head_block(k_ref, c, hd), head_block(v_ref, c, hd)
        s = lax.dot_general(qh, kh, (((1,), (1,)), ((), ())), preferred_element_type=F32)
        scores[c, hd] = (s * mask_ref[hd]).astype(BF16)
        kz = (kh.astype(F32) * zeta_ref[hd]).astype(BF16)
        kvs[c, hd] = lax.dot_general(kz, vh, (((0,), (0,)), ((), ())),
                                     preferred_element_type=F32)

    start_state = {}
    for hd in range(RET_HEADS):
        st = state[hd]
        for c in range(n_chunks):
            start_state[c, hd] = st.astype(BF16)
            st = st * cd_ref[hd] + kvs[c, hd]
        state[hd] = st

    for c, hd in blocks:
        r0, c0 = c * RET_CHUNK, hd * HEAD_DIM
        qh, vh = head_block(q_ref, c, hd), head_block(v_ref, c, hd)
        qx = (qh.astype(F32) * xi_ref[hd]).astype(BF16)
        lhs = jnp.concatenate([scores[c, hd], qx], axis=1)
        rhs = jnp.concatenate([vh, start_state[c, hd]], axis=0)
        o = _dot(lhs, rhs)
        mu = jnp.mean(o, axis=-1, keepdims=True)
        d = o - mu
        var = jnp.mean(d * d, axis=-1, keepdims=True)
        y = d * lax.rsqrt(var + LN_EPS) * gng_ref[:, c0:c0 + HEAD_DIM] + gnb_ref[:, c0:c0 + HEAD_DIM]
        g = head_block(gate_ref, c, hd).astype(F32)
        ret_buf[r0:r0 + RET_CHUNK, c0:c0 + HEAD_DIM] = (g * _sigmoid(g) * y).astype(BF16)

    mix = _dot(conv_ref[...], w_ref[0:CONV_CH, :]) + _dot(ret_buf[...], w_ref[CONV_CH:, :])
    o_ref[...] = x_ref[...] + mix


def _retention_constants():
    log_g = jnp.log(1.0 - jnp.exp2(-5.0 - jnp.arange(RET_HEADS, dtype=F32)))
    idx = jnp.arange(RET_CHUNK, dtype=F32)
    diff = idx[:, None] - idx[None, :]
    mask = jnp.where(diff[None] >= 0,
                     jnp.exp(jnp.maximum(diff, 0.0)[None] * log_g[:, None, None]), 0.0)
    zeta = jnp.exp((RET_CHUNK - 1 - idx)[None, :] * log_g[:, None])
    xi = jnp.exp((idx + 1.0)[None, :] * log_g[:, None])
    cd = jnp.exp(RET_CHUNK * log_g)
    full = (RET_HEADS, RET_CHUNK, HEAD_DIM)
    return (mask,
            jnp.broadcast_to(xi[:, :, None], full),
            jnp.broadcast_to(zeta[:, :, None], full),
            jnp.broadcast_to(cd[:, None, None], full))


def _mix_out(x, conv, q, k, v, gate, w_out, consts, gng, gnb):
    bsz, seq, _ = x.shape
    ts = TS_MIX_OUT
    tile = lambda b, s: (b, s, 0)
    const2 = lambda b, s: (0, 0)
    const3 = lambda b, s: (0, 0, 0)
    head_const = pl.BlockSpec((RET_HEADS, RET_CHUNK, HEAD_DIM), const3)
    branch = pl.BlockSpec((None, ts, RET_WIDTH), tile)
    return pl.pallas_call(
        _mix_out_kernel,
        grid=(bsz, seq // ts),
        in_specs=[pl.BlockSpec((None, ts, D_MODEL), tile),
                  branch, branch, branch, branch, branch,
                  pl.BlockSpec((D_MODEL, D_MODEL), const2),
                  head_const, head_const, head_const, head_const,
                  pl.BlockSpec((1, RET_WIDTH), const2),
                  pl.BlockSpec((1, RET_WIDTH), const2)],
        out_specs=pl.BlockSpec((None, ts, D_MODEL), tile),
        out_shape=jax.ShapeDtypeStruct(x.shape, F32),
        scratch_shapes=[pltpu.VMEM((RET_HEADS, HEAD_DIM, HEAD_DIM), F32),
                        pltpu.VMEM((ts, RET_WIDTH), BF16)],
        compiler_params=pltpu.CompilerParams(
            dimension_semantics=("arbitrary", "arbitrary"),
            vmem_limit_bytes=VMEM_LIMIT_BYTES),
        name="mix_out",
    )(x, conv, q, k, v, gate, w_out, *consts, gng, gnb)


def _ffn_ple_kernel(x_ref, p_ref, gf_ref, w1_ref, w2_ref, gp_ref, wg_ref, wp_ref, gl_ref,
                    o_ref, *, final_norm):
    x = x_ref[...]
    h = _rms_norm(x, gf_ref[...]).astype(BF16)
    acc = None
    for j in range(D_FF // FF_CHUNK):
        c0 = j * FF_CHUNK
        a = jnp.maximum(_dot(h, w1_ref[:, c0:c0 + FF_CHUNK]), 0.0)
        part = _dot((a * a).astype(BF16), w2_ref[c0:c0 + FF_CHUNK, :])
        acc = part if acc is None else acc + part
    x = x + acc
    h = _rms_norm(x, gp_ref[...]).astype(BF16)
    gate = _sigmoid(_dot(h, wg_ref[...]))
    x = x + gate * _dot(p_ref[...].astype(BF16), wp_ref[...])
    if final_norm:
        x = _rms_norm(x, gl_ref[...])
    o_ref[...] = x


def _ffn_ple(x, p, gf, w1, w2, gp, wg, wp, gl, final_norm):
    m = x.shape[0]
    tm = TM_FFN
    row = lambda i: (i, 0)
    const = lambda i: (0, 0)
    resident = lambda shape: pl.BlockSpec(shape, const, pipeline_mode=pl.Buffered(1))
    return pl.pallas_call(
        functools.partial(_ffn_ple_kernel, final_norm=final_norm),
        grid=(m // tm,),
        in_specs=[pl.BlockSpec((tm, D_MODEL), row),
                  pl.BlockSpec((tm, PLE_DIM), row),
                  pl.BlockSpec((1, D_MODEL), const),
                  resident((D_MODEL, D_FF)),
                  resident((D_FF, D_MODEL)),
                  pl.BlockSpec((1, D_MODEL), const),
                  resident((D_MODEL, D_MODEL)),
                  resident((PLE_DIM, D_MODEL)),
                  pl.BlockSpec((1, D_MODEL), const)],
        out_specs=pl.BlockSpec((tm, D_MODEL), row),
        out_shape=jax.ShapeDtypeStruct(x.shape, F32),
        compiler_params=pltpu.CompilerParams(
            dimension_semantics=("arbitrary",),
            vmem_limit_bytes=VMEM_LIMIT_BYTES),
        name="ffn_ple_final" if final_norm else "ffn_ple",
    )(x, p, gf, w1, w2, gp, wg, wp, gl)


def kernel(x, p, positions, norm_mix_g, w_in, conv_dw_w, conv_dw_b, conv_ln_g, conv_ln_b,
           ret_gn_g, ret_gn_b, w_out, norm_ffn_g, w_ff1, w_ff2, norm_ple_g,
           w_ple_gate, w_ple_proj, final_norm_g):
    bsz, seq, d = x.shape
    depth = w_in.shape[0]
    assert d == D_MODEL and seq % TM_MIX_IN == 0 and seq % TS_MIX_OUT == 0
    assert (bsz * seq) % TM_FFN == 0 and (bsz * seq) % TM_ROPE == 0

    cos_t, sin_t = _rope_tables(positions)
    cos_t = cos_t.reshape(bsz, seq, LANES)
    sin_t = sin_t.reshape(bsz, seq, LANES)
    consts = _retention_constants()
    row = lambda v: v[None, :]

    for i in range(depth):
        conv, q, k, v, gate = _mix_in(
            x, cos_t, sin_t, row(norm_mix_g[i]), w_in[i].astype(BF16),
            conv_dw_w[i], row(conv_dw_b[i]), row(conv_ln_g[i]), row(conv_ln_b[i]))
        x = _mix_out(x, conv, q, k, v, gate, w_out[i].astype(BF16), consts,
                     row(ret_gn_g[i]), row(ret_gn_b[i]))
        x = _ffn_ple(
            x.reshape(bsz * seq, d), p[i].reshape(bsz * seq, PLE_DIM),
            row(norm_ffn_g[i]), w_ff1[i].astype(BF16), w_ff2[i].astype(BF16),
            row(norm_ple_g[i]), w_ple_gate[i].astype(BF16), w_ple_proj[i].astype(BF16),
            row(final_norm_g), final_norm=(i == depth - 1)).reshape(bsz, seq, d)
    return x
```

```python
import functools

import jax
import jax.numpy as jnp
from jax import lax
from jax.experimental import pallas as pl
from jax.experimental.pallas import tpu as pltpu

D_MODEL = 1024
CONV_CH = 512
CONV_WIDTH = 31
RET_HEADS = 4
HEAD_DIM = 128
RET_WIDTH = RET_HEADS * HEAD_DIM
RET_CHUNK = 128
ROPE_BASE = 10000.0
D_FF = 4 * D_MODEL
PLE_DIM = 256
RMS_EPS = 1e-6
LN_EPS = 1e-5

LANES = 128
VMEM_LIMIT_BYTES = 56 * 1024 * 1024

HALO_ROWS = 32
CONV_ROW_BLOCK = 64
CONV_CHAINS = 1
PROJ_PIECE = 256
TM_MIXER = 512
TM_FFN = 1024
FF_CHUNK = 1024
TM_ROPE = 2048

F32 = jnp.float32
BF16 = jnp.bfloat16


def _dot(a, b):
    return jnp.dot(a, b, preferred_element_type=F32)


def _rms_norm(x, g):
    ms = jnp.mean(x * x, axis=-1, keepdims=True)
    return x * lax.rsqrt(ms + RMS_EPS) * g


def _sigmoid(x):
    return 1.0 / (1.0 + jnp.exp(-x))


def _rope_kernel(pos_ref, invf_ref, sign_ref, cos_ref, sin_ref):
    half = pos_ref.shape[0] // 2
    lo = lax.broadcasted_iota(jnp.int32, (half, LANES), 1) < HEAD_DIM // 2
    pos = jnp.where(lo, pos_ref[0:half, :], pos_ref[half:, :]).astype(F32)
    ang = pos * invf_ref[...]
    for fn, dst, sign in ((jnp.cos, cos_ref, None), (jnp.sin, sin_ref, sign_ref[...])):
        t = fn(ang)
        t_sw = pltpu.roll(t, HEAD_DIM // 2, 1)
        first, second = jnp.where(lo, t, t_sw), jnp.where(lo, t_sw, t)
        dst[0:half, :] = first if sign is None else first * sign
        dst[half:, :] = second if sign is None else second * sign


def _rope_tables(positions):
    m = positions.size
    half = HEAD_DIM // 2
    inv_freq = 1.0 / (ROPE_BASE ** (jnp.arange(0, half, dtype=F32) * (2.0 / HEAD_DIM)))
    invf = jnp.concatenate([inv_freq, inv_freq])[None, :]
    sign = jnp.concatenate([-jnp.ones((half,), F32), jnp.ones((half,), F32)])[None, :]
    pos = positions.reshape(m, 1)
    row = lambda i: (i, 0)
    const = lambda i: (0, 0)
    return pl.pallas_call(
        _rope_kernel,
        grid=(m // TM_ROPE,),
        in_specs=[pl.BlockSpec((TM_ROPE, 1), row),
                  pl.BlockSpec((1, LANES), const),
                  pl.BlockSpec((1, LANES), const)],
        out_specs=[pl.BlockSpec((TM_ROPE, LANES), row),
                   pl.BlockSpec((TM_ROPE, LANES), row)],
        out_shape=[jax.ShapeDtypeStruct((m, LANES), F32)] * 2,
        compiler_params=pltpu.CompilerParams(dimension_semantics=("arbitrary",)),
        name="rope_tables",
    )(pos, invf, sign)


def _head_block(ref, c, hd):
    return ref[c * RET_CHUNK:(c + 1) * RET_CHUNK, hd * HEAD_DIM:(hd + 1) * HEAD_DIM]


def _mixer_kernel(x_ref, cos_ref, sin_ref, g_ref, w_ref, cw_ref, cb_ref, lng_ref, lnb_ref,
                  wout_ref, mask_ref, xi_ref, zeta_ref, cd_ref, gng_ref, gnb_ref,
                  o_ref,
                  hbuf, state, h_buf, q_buf, k_buf, v_buf, gate_buf, conv_buf, ret_buf,
                  sc_buf, st_buf):
    tm = x_ref.shape[0]
    n_cb = CONV_CH // LANES
    n_chunks = tm // RET_CHUNK
    blocks = [(c, hd) for c in range(n_chunks) for hd in range(RET_HEADS)]

    @pl.when(pl.program_id(1) == 0)
    def _():
        hbuf[:, 0:2 * HALO_ROWS, :] = jnp.zeros((n_cb, 2 * HALO_ROWS, LANES), F32)
        state[...] = jnp.zeros(state.shape, F32)

    def stage(fn):
        fn()

    def conv_rows(rb):
        first = HALO_ROWS - (CONV_WIDTH - 1)
        r0 = rb * CONV_ROW_BLOCK
        accs = []
        for cb in range(n_cb):
            c0 = cb * LANES
            parts = [None] * CONV_CHAINS
            for t in range(CONV_WIDTH):
                seg = hbuf[cb, pl.ds(2 * (first + r0 + t), CONV_ROW_BLOCK, stride=2), :]
                term = seg * cw_ref[t:t + 1, c0:c0 + LANES]
                parts[t % CONV_CHAINS] = term if parts[t % CONV_CHAINS] is None else parts[t % CONV_CHAINS] + term
            acc = parts[0]
            for part in parts[1:]:
                acc = acc + part
            accs.append(acc + cb_ref[:, c0:c0 + LANES])
        tot = accs[0]
        for cb in range(1, n_cb):
            tot = tot + accs[cb]
        mu = jnp.sum(tot, axis=-1, keepdims=True) * (1.0 / CONV_CH)
        devs = [acc - mu for acc in accs]
        sq = devs[0] * devs[0]
        for cb in range(1, n_cb):
            sq = sq + devs[cb] * devs[cb]
        var = jnp.sum(sq, axis=-1, keepdims=True) * (1.0 / CONV_CH)
        inv = lax.rsqrt(var + LN_EPS)
        for cb in range(n_cb):
            c0 = cb * LANES
            y = devs[cb] * inv * lng_ref[:, c0:c0 + LANES] + lnb_ref[:, c0:c0 + LANES]
            conv_buf[r0:r0 + CONV_ROW_BLOCK, c0:c0 + LANES] = (y * _sigmoid(y)).astype(BF16)

    def rotary(z, scale):
        cos = cos_ref[...]
        sin = sin_ref[...]
        heads = []
        for hd in range(z.shape[1] // HEAD_DIM):
            zh = z[:, hd * HEAD_DIM:(hd + 1) * HEAD_DIM]
            zr = zh * cos + pltpu.roll(zh, HEAD_DIM // 2, 1) * sin
            heads.append(zr if scale is None else zr * scale)
        return jnp.concatenate(heads, axis=1)

    h_buf[...] = _rms_norm(x_ref[...], g_ref[...]).astype(BF16)
    a = _dot(h_buf[...], w_ref[:, 0:CONV_CH])
    b = _dot(h_buf[...], w_ref[:, CONV_CH:2 * CONV_CH])
    glu = a * _sigmoid(b)
    for cb in range(n_cb):
        hbuf[cb, pl.ds(2 * HALO_ROWS, tm, stride=2), :] = glu[:, cb * LANES:(cb + 1) * LANES]

    q_col = 2 * CONV_CH

    finish = (lambda z: rotary(z, None).astype(BF16),
              lambda z: rotary(z, HEAD_DIM ** -0.5).astype(BF16),
              lambda z: z.astype(BF16),
              lambda z: z * _sigmoid(z))
    dsts = (q_buf, k_buf, v_buf, gate_buf)
    n_pieces = len(dsts) * (RET_WIDTH // PROJ_PIECE)
    n_rb = tm // CONV_ROW_BLOCK
    for i in range(n_pieces):
        j, c0 = divmod(i * PROJ_PIECE, RET_WIDTH)
        z = _dot(h_buf[...], w_ref[:, q_col + j * RET_WIDTH + c0:q_col + j * RET_WIDTH + c0 + PROJ_PIECE])
        dsts[j][:, c0:c0 + PROJ_PIECE] = finish[j](z)
        for rb in range(i * n_rb // n_pieces, (i + 1) * n_rb // n_pieces):
            conv_rows(rb)

    half_d = D_MODEL // 2

    @stage
    def _():
        kvs = {}
        for i, (c, hd) in enumerate(blocks):
            qh, kh, vh = _head_block(q_buf, c, hd), _head_block(k_buf, c, hd), _head_block(v_buf, c, hd)
            s = lax.dot_general(qh, kh, (((1,), (1,)), ((), ())), preferred_element_type=F32)
            sc_buf[i] = (s * mask_ref[hd]).astype(BF16)
            kz = (kh.astype(F32) * zeta_ref[hd]).astype(BF16)
            kvs[c, hd] = lax.dot_general(kz, vh, (((0,), (0,)), ((), ())),
                                         preferred_element_type=F32)
        for hd in range(RET_HEADS):
            st = state[hd]
            for c in range(n_chunks):
                st_buf[c * RET_HEADS + hd] = st.astype(BF16)
                st = st * cd_ref[hd] + kvs[c, hd]
            state[hd] = st
        o_ref[:, 0:half_d] = x_ref[:, 0:half_d] + _dot(conv_buf[...], wout_ref[0:CONV_CH, 0:half_d])

    @stage
    def _():
        for i, (c, hd) in enumerate(blocks):
            r0, c0 = c * RET_CHUNK, hd * HEAD_DIM
            qx = (_head_block(q_buf, c, hd).astype(F32) * xi_ref[hd]).astype(BF16)
            lhs = jnp.concatenate([sc_buf[i], qx], axis=1)
            rhs = jnp.concatenate([_head_block(v_buf, c, hd), st_buf[i]], axis=0)
            o = _dot(lhs, rhs)
            mu = jnp.mean(o, axis=-1, keepdims=True)
            d = o - mu
            var = jnp.mean(d * d, axis=-1, keepdims=True)
            y = d * lax.rsqrt(var + LN_EPS) * gng_ref[:, c0:c0 + HEAD_DIM] + gnb_ref[:, c0:c0 + HEAD_DIM]
            ret_buf[r0:r0 + RET_CHUNK, c0:c0 + HEAD_DIM] = (_head_block(gate_buf, c, hd) * y).astype(BF16)
        o_ref[:, half_d:] = x_ref[:, half_d:] + _dot(conv_buf[...], wout_ref[0:CONV_CH, half_d:])

    @stage
    def _():
        o_ref[...] += _dot(ret_buf[...], wout_ref[CONV_CH:, :])
        for cb in range(n_cb):
            hbuf[cb, pl.ds(0, HALO_ROWS, stride=2), :] = hbuf[cb, pl.ds(2 * tm, HALO_ROWS, stride=2), :]


def _retention_constants():
    log_g = jnp.log(1.0 - jnp.exp2(-5.0 - jnp.arange(RET_HEADS, dtype=F32)))
    idx = jnp.arange(RET_CHUNK, dtype=F32)
    diff = idx[:, None] - idx[None, :]
    mask = jnp.where(diff[None] >= 0,
                     jnp.exp(jnp.maximum(diff, 0.0)[None] * log_g[:, None, None]), 0.0)
    zeta = jnp.exp((RET_CHUNK - 1 - idx)[None, :] * log_g[:, None])
    xi = jnp.exp((idx + 1.0)[None, :] * log_g[:, None])
    cd = jnp.exp(RET_CHUNK * log_g)
    full = (RET_HEADS, RET_CHUNK, HEAD_DIM)
    return (mask,
            jnp.broadcast_to(xi[:, :, None], full),
            jnp.broadcast_to(zeta[:, :, None], full),
            jnp.broadcast_to(cd[:, None, None], full))


def _mixer(x, cos_t, sin_t, norm_g, w_in, cw, cb, lng, lnb, w_out, consts, gng, gnb):
    bsz, seq, _ = x.shape
    tm = TM_MIXER
    n_blocks = (tm // RET_CHUNK) * RET_HEADS
    tile = lambda b, s: (b, s, 0)
    const2 = lambda b, s: (0, 0)
    const3 = lambda b, s: (0, 0, 0)
    resident = lambda shape: pl.BlockSpec(shape, const2, pipeline_mode=pl.Buffered(1))
    head_const = pl.BlockSpec((RET_HEADS, RET_CHUNK, HEAD_DIM), const3)
    return pl.pallas_call(
        _mixer_kernel,
        grid=(bsz, seq // tm),
        in_specs=[pl.BlockSpec((None, tm, D_MODEL), tile),
                  pl.BlockSpec((None, tm, LANES), tile),
                  pl.BlockSpec((None, tm, LANES), tile),
                  pl.BlockSpec((1, D_MODEL), const2),
                  resident(w_in.shape),
                  pl.BlockSpec((CONV_WIDTH, CONV_CH), const2),
                  pl.BlockSpec((1, CONV_CH), const2),
                  pl.BlockSpec((1, CONV_CH), const2),
                  pl.BlockSpec((1, CONV_CH), const2),
                  resident(w_out.shape),
                  head_const, head_const, head_const, head_const,
                  pl.BlockSpec((1, RET_WIDTH), const2),
                  pl.BlockSpec((1, RET_WIDTH), const2)],
        out_specs=pl.BlockSpec((None, tm, D_MODEL), tile),
        out_shape=jax.ShapeDtypeStruct(x.shape, F32),
        scratch_shapes=[
            pltpu.VMEM((CONV_CH // LANES, 2 * (HALO_ROWS + tm), LANES), F32),
            pltpu.VMEM((RET_HEADS, HEAD_DIM, HEAD_DIM), F32),
            pltpu.VMEM((tm, D_MODEL), BF16),
            pltpu.VMEM((tm, RET_WIDTH), BF16),
            pltpu.VMEM((tm, RET_WIDTH), BF16),
            pltpu.VMEM((tm, RET_WIDTH), BF16),
            pltpu.VMEM((tm, RET_WIDTH), F32),
            pltpu.VMEM((tm, CONV_CH), BF16),
            pltpu.VMEM((tm, RET_WIDTH), BF16),
            pltpu.VMEM((n_blocks, RET_CHUNK, RET_CHUNK), BF16),
            pltpu.VMEM((n_blocks, HEAD_DIM, HEAD_DIM), BF16),
        ],
        compiler_params=pltpu.CompilerParams(
            dimension_semantics=("arbitrary", "arbitrary"),
            vmem_limit_bytes=VMEM_LIMIT_BYTES),
        name="mixer",
    )(x, cos_t, sin_t, norm_g, w_in, cw, cb, lng, lnb, w_out, *consts, gng, gnb)


def _ffn_ple_kernel(x_ref, p_ref, gf_ref, w1_ref, w2_ref, gp_ref, wg_ref, wp_ref, gl_ref,
                    o_ref, *, final_norm):
    x = x_ref[...]
    h = _rms_norm(x, gf_ref[...]).astype(BF16)
    acc = None
    for j in range(D_FF // FF_CHUNK):
        c0 = j * FF_CHUNK
        a = jnp.maximum(_dot(h, w1_ref[:, c0:c0 + FF_CHUNK]), 0.0)
        part = _dot((a * a).astype(BF16), w2_ref[c0:c0 + FF_CHUNK, :])
        acc = part if acc is None else acc + part
    x = x + acc
    h = _rms_norm(x, gp_ref[...]).astype(BF16)
    gate = _sigmoid(_dot(h, wg_ref[...]))
    x = x + gate * _dot(p_ref[...].astype(BF16), wp_ref[...])
    if final_norm:
        x = _rms_norm(x, gl_ref[...])
    o_ref[...] = x


def _ffn_ple(x, p, layer, gf, w1, w2, gp, wg, wp, gl, final_norm):
    m = x.shape[0]
    tm = TM_FFN
    row = lambda i: (i, 0)
    const = lambda i: (0, 0)
    resident = lambda shape: pl.BlockSpec(shape, const, pipeline_mode=pl.Buffered(1))
    return pl.pallas_call(
        functools.partial(_ffn_ple_kernel, final_norm=final_norm),
        grid=(m // tm,),
        in_specs=[pl.BlockSpec((tm, D_MODEL), row),
                  pl.BlockSpec((None, tm, PLE_DIM), lambda i: (layer, i, 0)),
                  pl.BlockSpec((1, D_MODEL), const),
                  resident((D_MODEL, D_FF)),
                  resident((D_FF, D_MODEL)),
                  pl.BlockSpec((1, D_MODEL), const),
                  resident((D_MODEL, D_MODEL)),
                  resident((PLE_DIM, D_MODEL)),
                  pl.BlockSpec((1, D_MODEL), const)],
        out_specs=pl.BlockSpec((tm, D_MODEL), row),
        out_shape=jax.ShapeDtypeStruct(x.shape, F32),
        compiler_params=pltpu.CompilerParams(
            dimension_semantics=("arbitrary",),
            vmem_limit_bytes=VMEM_LIMIT_BYTES),
        name="ffn_ple_final" if final_norm else "ffn_ple",
    )(x, p, gf, w1, w2, gp, wg, wp, gl)


def kernel(x, p, positions, norm_mix_g, w_in, conv_dw_w, conv_dw_b, conv_ln_g, conv_ln_b,
           ret_gn_g, ret_gn_b, w_out, norm_ffn_g, w_ff1, w_ff2, norm_ple_g,
           w_ple_gate, w_ple_proj, final_norm_g):
    bsz, seq, d = x.shape
    depth = w_in.shape[0]
    assert d == D_MODEL and seq % TM_MIXER == 0
    assert (bsz * seq) % TM_FFN == 0 and (bsz * seq) % TM_ROPE == 0

    cos_t, sin_t = _rope_tables(positions)
    cos_t = cos_t.reshape(bsz, seq, LANES)
    sin_t = sin_t.reshape(bsz, seq, LANES)
    consts = _retention_constants()
    row = lambda v: v[None, :]

    for i in range(depth):
        x = _mixer(
            x, cos_t, sin_t, row(norm_mix_g[i]), w_in[i].astype(BF16),
            conv_dw_w[i], row(conv_dw_b[i]), row(conv_ln_g[i]), row(conv_ln_b[i]),
            w_out[i].astype(BF16), consts, row(ret_gn_g[i]), row(ret_gn_b[i]))
        x = _ffn_ple(
            x.reshape(bsz * seq, d), p.reshape(depth, bsz * seq, PLE_DIM), i,
            row(norm_ffn_g[i]), w_ff1[i].astype(BF16), w_ff2[i].astype(BF16),
            row(norm_ple_g[i]), w_ple_gate[i].astype(BF16), w_ple_proj[i].astype(BF16),
            row(final_norm_g), final_norm=(i == depth - 1)).reshape(bsz, seq, d)
    return x
```

```python
import functools

import jax
import jax.numpy as jnp
from jax import lax
from jax.experimental import pallas as pl
from jax.experimental.pallas import tpu as pltpu

D_MODEL = 1024
CONV_CH = 512
CONV_WIDTH = 31
RET_HEADS = 4
HEAD_DIM = 128
RET_WIDTH = RET_HEADS * HEAD_DIM
RET_CHUNK = 128
ROPE_BASE = 10000.0
D_FF = 4 * D_MODEL
PLE_DIM = 256
RMS_EPS = 1e-6
LN_EPS = 1e-5

LANES = 128
VMEM_LIMIT_BYTES = 56 * 1024 * 1024

HALO_ROWS = 32
CONV_ROW_BLOCK = 64
CONV_CHAINS = 1
PROJ_PIECE = 256
TM_MIXER = 1024
TM_FFN = 1024
FF_CHUNK = 1024
TM_ROPE = 2048

F32 = jnp.float32
BF16 = jnp.bfloat16


def _dot(a, b):
    return jnp.dot(a, b, preferred_element_type=F32)


def _rms_norm(x, g):
    ms = jnp.mean(x * x, axis=-1, keepdims=True)
    return x * lax.rsqrt(ms + RMS_EPS) * g


def _sigmoid(x):
    return 1.0 / (1.0 + jnp.exp(-x))


def _rope_kernel(pos_ref, invf_ref, sign_ref, cos_ref, sin_ref):
    half = pos_ref.shape[0] // 2
    lo = lax.broadcasted_iota(jnp.int32, (half, LANES), 1) < HEAD_DIM // 2
    pos = jnp.where(lo, pos_ref[0:half, :], pos_ref[half:, :]).astype(F32)
    ang = pos * invf_ref[...]
    for fn, dst, sign in ((jnp.cos, cos_ref, None), (jnp.sin, sin_ref, sign_ref[...])):
        t = fn(ang)
        t_sw = pltpu.roll(t, HEAD_DIM // 2, 1)
        first, second = jnp.where(lo, t, t_sw), jnp.where(lo, t_sw, t)
        dst[0:half, :] = first if sign is None else first * sign
        dst[half:, :] = second if sign is None else second * sign


def _rope_tables(positions):
    m = positions.size
    half = HEAD_DIM // 2
    inv_freq = 1.0 / (ROPE_BASE ** (jnp.arange(0, half, dtype=F32) * (2.0 / HEAD_DIM)))
    invf = jnp.concatenate([inv_freq, inv_freq])[None, :]
    sign = jnp.concatenate([-jnp.ones((half,), F32), jnp.ones((half,), F32)])[None, :]
    pos = positions.reshape(m, 1)
    row = lambda i: (i, 0)
    const = lambda i: (0, 0)
    return pl.pallas_call(
        _rope_kernel,
        grid=(m // TM_ROPE,),
        in_specs=[pl.BlockSpec((TM_ROPE, 1), row),
                  pl.BlockSpec((1, LANES), const),
                  pl.BlockSpec((1, LANES), const)],
        out_specs=[pl.BlockSpec((TM_ROPE, LANES), row),
                   pl.BlockSpec((TM_ROPE, LANES), row)],
        out_shape=[jax.ShapeDtypeStruct((m, LANES), F32)] * 2,
        compiler_params=pltpu.CompilerParams(dimension_semantics=("arbitrary",)),
        name="rope_tables",
    )(pos, invf, sign)


def _head_block(ref, c, hd):
    return ref[c * RET_CHUNK:(c + 1) * RET_CHUNK, hd * HEAD_DIM:(hd + 1) * HEAD_DIM]


def _mixer_kernel(x_ref, cos_ref, sin_ref, g_ref, w_ref, cw_ref, cb_ref, lng_ref, lnb_ref,
                  wout_ref, mask_ref, xi_ref, zeta_ref, cd_ref, gng_ref, gnb_ref,
                  o_ref,
                  hbuf, state, h_buf, q_buf, k_buf, v_buf, gate_buf, conv_buf, ret_buf,
                  sc_buf, st_buf):
    tm = x_ref.shape[0]
    n_cb = CONV_CH // LANES
    n_chunks = tm // RET_CHUNK
    blocks = [(c, hd) for c in range(n_chunks) for hd in range(RET_HEADS)]

    @pl.when(pl.program_id(1) == 0)
    def _():
        hbuf[:, 0:2 * HALO_ROWS, :] = jnp.zeros((n_cb, 2 * HALO_ROWS, LANES), F32)
        state[...] = jnp.zeros(state.shape, F32)

    def stage(fn):
        fn()

    def conv_rows(rb):
        first = HALO_ROWS - (CONV_WIDTH - 1)
        r0 = rb * CONV_ROW_BLOCK
        accs = []
        for cb in range(n_cb):
            c0 = cb * LANES
            parts = [None] * CONV_CHAINS
            for t in range(CONV_WIDTH):
                seg = hbuf[cb, pl.ds(2 * (first + r0 + t), CONV_ROW_BLOCK, stride=2), :]
                term = seg * cw_ref[t:t + 1, c0:c0 + LANES]
                parts[t % CONV_CHAINS] = term if parts[t % CONV_CHAINS] is None else parts[t % CONV_CHAINS] + term
            acc = parts[0]
            for part in parts[1:]:
                acc = acc + part
            accs.append(acc + cb_ref[:, c0:c0 + LANES])
        tot = accs[0]
        for cb in range(1, n_cb):
            tot = tot + accs[cb]
        mu = jnp.sum(tot, axis=-1, keepdims=True) * (1.0 / CONV_CH)
        devs = [acc - mu for acc in accs]
        sq = devs[0] * devs[0]
        for cb in range(1, n_cb):
            sq = sq + devs[cb] * devs[cb]
        var = jnp.sum(sq, axis=-1, keepdims=True) * (1.0 / CONV_CH)
        inv = lax.rsqrt(var + LN_EPS)
        for cb in range(n_cb):
            c0 = cb * LANES
            y = devs[cb] * inv * lng_ref[:, c0:c0 + LANES] + lnb_ref[:, c0:c0 + LANES]
            conv_buf[r0:r0 + CONV_ROW_BLOCK, c0:c0 + LANES] = (y * _sigmoid(y)).astype(BF16)

    def rotary(z, scale):
        cos = cos_ref[...]
        sin = sin_ref[...]
        heads = []
        for hd in range(z.shape[1] // HEAD_DIM):
            zh = z[:, hd * HEAD_DIM:(hd + 1) * HEAD_DIM]
            zr = zh * cos + pltpu.roll(zh, HEAD_DIM // 2, 1) * sin
            heads.append(zr if scale is None else zr * scale)
        return jnp.concatenate(heads, axis=1)

    h_buf[...] = _rms_norm(x_ref[...], g_ref[...]).astype(BF16)
    a = _dot(h_buf[...], w_ref[:, 0:CONV_CH])
    b = _dot(h_buf[...], w_ref[:, CONV_CH:2 * CONV_CH])
    glu = a * _sigmoid(b)
    for cb in range(n_cb):
        hbuf[cb, pl.ds(2 * HALO_ROWS, tm, stride=2), :] = glu[:, cb * LANES:(cb + 1) * LANES]

    q_col = 2 * CONV_CH

    finish = (lambda z: rotary(z, None).astype(BF16),
              lambda z: rotary(z, HEAD_DIM ** -0.5).astype(BF16),
              lambda z: z.astype(BF16),
              lambda z: z * _sigmoid(z))
    dsts = (q_buf, k_buf, v_buf, gate_buf)
    n_pieces = len(dsts) * (RET_WIDTH // PROJ_PIECE)
    n_rb = tm // CONV_ROW_BLOCK
    for i in range(n_pieces):
        j, c0 = divmod(i * PROJ_PIECE, RET_WIDTH)
        z = _dot(h_buf[...], w_ref[:, q_col + j * RET_WIDTH + c0:q_col + j * RET_WIDTH + c0 + PROJ_PIECE])
        dsts[j][:, c0:c0 + PROJ_PIECE] = finish[j](z)
        for rb in range(i * n_rb // n_pieces, (i + 1) * n_rb // n_pieces):
            conv_rows(rb)

    half_d = D_MODEL // 2

    @stage
    def _():
        kvs = {}
        for i, (c, hd) in enumerate(blocks):
            qh, kh, vh = _head_block(q_buf, c, hd), _head_block(k_buf, c, hd), _head_block(v_buf, c, hd)
            s = lax.dot_general(qh, kh, (((1,), (1,)), ((), ())), preferred_element_type=F32)
            sc_buf[i] = (s * mask_ref[hd]).astype(BF16)
            kz = (kh.astype(F32) * zeta_ref[hd]).astype(BF16)
            kvs[c, hd] = lax.dot_general(kz, vh, (((0,), (0,)), ((), ())),
                                         preferred_element_type=F32)
        for hd in range(RET_HEADS):
            st = state[hd]
            for c in range(n_chunks):
                st_buf[c * RET_HEADS + hd] = st.astype(BF16)
                st = st * cd_ref[hd] + kvs[c, hd]
            state[hd] = st

    @stage
    def _():
        for i, (c, hd) in enumerate(blocks):
            r0, c0 = c * RET_CHUNK, hd * HEAD_DIM
            qx = (_head_block(q_buf, c, hd).astype(F32) * xi_ref[hd]).astype(BF16)
            lhs = jnp.concatenate([sc_buf[i], qx], axis=1)
            rhs = jnp.concatenate([_head_block(v_buf, c, hd), st_buf[i]], axis=0)
            o = _dot(lhs, rhs)
            mu = jnp.mean(o, axis=-1, keepdims=True)
            d = o - mu
            var = jnp.mean(d * d, axis=-1, keepdims=True)
            y = d * lax.rsqrt(var + LN_EPS) * gng_ref[:, c0:c0 + HEAD_DIM] + gnb_ref[:, c0:c0 + HEAD_DIM]
            ret_buf[r0:r0 + RET_CHUNK, c0:c0 + HEAD_DIM] = (_head_block(gate_buf, c, hd) * y).astype(BF16)

    @stage
    def _():
        o_ref[...] = x_ref[...] + _dot(ret_buf[...], wout_ref[CONV_CH:, :])
        for r0 in range(0, tm, tm // 2):
            o_ref[r0:r0 + tm // 2, :] += _dot(conv_buf[r0:r0 + tm // 2, :], wout_ref[0:CONV_CH, :])
        for cb in range(n_cb):
            hbuf[cb, pl.ds(0, HALO_ROWS, stride=2), :] = hbuf[cb, pl.ds(2 * tm, HALO_ROWS, stride=2), :]


def _retention_constants():
    log_g = jnp.log(1.0 - jnp.exp2(-5.0 - jnp.arange(RET_HEADS, dtype=F32)))
    idx = jnp.arange(RET_CHUNK, dtype=F32)
    diff = idx[:, None] - idx[None, :]
    mask = jnp.where(diff[None] >= 0,
                     jnp.exp(jnp.maximum(diff, 0.0)[None] * log_g[:, None, None]), 0.0)
    zeta = jnp.exp((RET_CHUNK - 1 - idx)[None, :] * log_g[:, None])
    xi = jnp.exp((idx + 1.0)[None, :] * log_g[:, None])
    cd = jnp.exp(RET_CHUNK * log_g)
    full = (RET_HEADS, RET_CHUNK, HEAD_DIM)
    return (mask,
            jnp.broadcast_to(xi[:, :, None], full),
            jnp.broadcast_to(zeta[:, :, None], full),
            jnp.broadcast_to(cd[:, None, None], full))


def _mixer(x, cos_t, sin_t, norm_g, w_in, cw, cb, lng, lnb, w_out, consts, gng, gnb):
    bsz, seq, _ = x.shape
    tm = TM_MIXER
    n_blocks = (tm // RET_CHUNK) * RET_HEADS
    tile = lambda b, s: (b, s, 0)
    const2 = lambda b, s: (0, 0)
    const3 = lambda b, s: (0, 0, 0)
    resident = lambda shape: pl.BlockSpec(shape, const2, pipeline_mode=pl.Buffered(1))
    head_const = pl.BlockSpec((RET_HEADS, RET_CHUNK, HEAD_DIM), const3)
    return pl.pallas_call(
        _mixer_kernel,
        grid=(bsz, seq // tm),
        in_specs=[pl.BlockSpec((None, tm, D_MODEL), tile),
                  pl.BlockSpec((None, tm, LANES), tile),
                  pl.BlockSpec((None, tm, LANES), tile),
                  pl.BlockSpec((1, D_MODEL), const2),
                  resident(w_in.shape),
                  pl.BlockSpec((CONV_WIDTH, CONV_CH), const2),
                  pl.BlockSpec((1, CONV_CH), const2),
                  pl.BlockSpec((1, CONV_CH), const2),
                  pl.BlockSpec((1, CONV_CH), const2),
                  resident(w_out.shape),
                  head_const, head_const, head_const, head_const,
                  pl.BlockSpec((1, RET_WIDTH), const2),
                  pl.BlockSpec((1, RET_WIDTH), const2)],
        out_specs=pl.BlockSpec((None, tm, D_MODEL), tile),
        out_shape=jax.ShapeDtypeStruct(x.shape, F32),
        scratch_shapes=[
            pltpu.VMEM((CONV_CH // LANES, 2 * (HALO_ROWS + tm), LANES), F32),
            pltpu.VMEM((RET_HEADS, HEAD_DIM, HEAD_DIM), F32),
            pltpu.VMEM((tm, D_MODEL), BF16),
            pltpu.VMEM((tm, RET_WIDTH), BF16),
            pltpu.VMEM((tm, RET_WIDTH), BF16),
            pltpu.VMEM((tm, RET_WIDTH), BF16),
            pltpu.VMEM((tm, RET_WIDTH), F32),
            pltpu.VMEM((tm, CONV_CH), BF16),
            pltpu.VMEM((tm, RET_WIDTH), BF16),
            pltpu.VMEM((n_blocks, RET_CHUNK, RET_CHUNK), BF16),
            pltpu.VMEM((n_blocks, HEAD_DIM, HEAD_DIM), BF16),
        ],
        compiler_params=pltpu.CompilerParams(
            dimension_semantics=("arbitrary", "arbitrary"),
            vmem_limit_bytes=VMEM_LIMIT_BYTES),
        name="mixer",
    )(x, cos_t, sin_t, norm_g, w_in, cw, cb, lng, lnb, w_out, *consts, gng, gnb)


def _ffn_ple_kernel(x_ref, p_ref, gf_ref, w1_ref, w2_ref, gp_ref, wg_ref, wp_ref, gl_ref,
                    o_ref, *, final_norm):
    x = x_ref[...]
    h = _rms_norm(x, gf_ref[...]).astype(BF16)
    acc = None
    for j in range(D_FF // FF_CHUNK):
        c0 = j * FF_CHUNK
        a = jnp.maximum(_dot(h, w1_ref[:, c0:c0 + FF_CHUNK]), 0.0)
        part = _dot((a * a).astype(BF16), w2_ref[c0:c0 + FF_CHUNK, :])
        acc = part if acc is None else acc + part
    x = x + acc
    h = _rms_norm(x, gp_ref[...]).astype(BF16)
    gate = _sigmoid(_dot(h, wg_ref[...]))
    x = x + gate * _dot(p_ref[...].astype(BF16), wp_ref[...])
    if final_norm:
        x = _rms_norm(x, gl_ref[...])
    o_ref[...] = x


def _ffn_ple(x, p, layer, gf, w1, w2, gp, wg, wp, gl, final_norm):
    m = x.shape[0]
    tm = TM_FFN
    row = lambda i: (i, 0)
    const = lambda i: (0, 0)
    resident = lambda shape: pl.BlockSpec(shape, const, pipeline_mode=pl.Buffered(1))
    return pl.pallas_call(
        functools.partial(_ffn_ple_kernel, final_norm=final_norm),
        grid=(m // tm,),
        in_specs=[pl.BlockSpec((tm, D_MODEL), row),
                  pl.BlockSpec((None, tm, PLE_DIM), lambda i: (layer, i, 0)),
                  pl.BlockSpec((1, D_MODEL), const),
                  resident((D_MODEL, D_FF)),
                  resident((D_FF, D_MODEL)),
                  pl.BlockSpec((1, D_MODEL), const),
                  resident((D_MODEL, D_MODEL)),
                  resident((PLE_DIM, D_MODEL)),
                  pl.BlockSpec((1, D_MODEL), const)],
        out_specs=pl.BlockSpec((tm, D_MODEL), row),
        out_shape=jax.ShapeDtypeStruct(x.shape, F32),
        compiler_params=pltpu.CompilerParams(
            dimension_semantics=("arbitrary",),
            vmem_limit_bytes=VMEM_LIMIT_BYTES),
        name="ffn_ple_final" if final_norm else "ffn_ple",
    )(x, p, gf, w1, w2, gp, wg, wp, gl)


def kernel(x, p, positions, norm_mix_g, w_in, conv_dw_w, conv_dw_b, conv_ln_g, conv_ln_b,
           ret_gn_g, ret_gn_b, w_out, norm_ffn_g, w_ff1, w_ff2, norm_ple_g,
           w_ple_gate, w_ple_proj, final_norm_g):
    bsz, seq, d = x.shape
    depth = w_in.shape[0]
    assert d == D_MODEL and seq % TM_MIXER == 0
    assert (bsz * seq) % TM_FFN == 0 and (bsz * seq) % TM_ROPE == 0

    cos_t, sin_t = _rope_tables(positions)
    cos_t = cos_t.reshape(bsz, seq, LANES)
    sin_t = sin_t.reshape(bsz, seq, LANES)
    consts = _retention_constants()
    row = lambda v: v[None, :]

    for i in range(depth):
        x = _mixer(
            x, cos_t, sin_t, row(norm_mix_g[i]), w_in[i].astype(BF16),
            conv_dw_w[i], row(conv_dw_b[i]), row(conv_ln_g[i]), row(conv_ln_b[i]),
            w_out[i].astype(BF16), consts, row(ret_gn_g[i]), row(ret_gn_b[i]))
        x = _ffn_ple(
            x.reshape(bsz * seq, d), p.reshape(depth, bsz * seq, PLE_DIM), i,
            row(norm_ffn_g[i]), w_ff1[i].astype(BF16), w_ff2[i].astype(BF16),
            row(norm_ple_g[i]), w_ple_gate[i].astype(BF16), w_ple_proj[i].astype(BF16),
            row(final_norm_g), final_norm=(i == depth - 1)).reshape(bsz, seq, d)
    return x
```

```python
import functools

import jax
import jax.numpy as jnp
from jax import lax
from jax.experimental import pallas as pl
from jax.experimental.pallas import tpu as pltpu

D_MODEL = 1024
CONV_CH = 512
CONV_WIDTH = 31
RET_HEADS = 4
HEAD_DIM = 128
RET_WIDTH = RET_HEADS * HEAD_DIM
RET_CHUNK = 128
ROPE_BASE = 10000.0
D_FF = 4 * D_MODEL
PLE_DIM = 256
RMS_EPS = 1e-6
LN_EPS = 1e-5

LANES = 128
VMEM_LIMIT_BYTES = 56 * 1024 * 1024

HALO_ROWS = 32
CONV_ROW_BLOCK = 64
PROJ_PIECE = 256
TM_MIXER = 1024
TM_FFN = 1024
FFN_SUB = 512
FF_CHUNK = 1024
TM_ROPE = 2048

F32 = jnp.float32
BF16 = jnp.bfloat16


def _dot(a, b):
    return jnp.dot(a, b, preferred_element_type=F32)


def _rms_norm(x, g):
    ms = jnp.mean(x * x, axis=-1, keepdims=True)
    return x * lax.rsqrt(ms + RMS_EPS) * g


def _sigmoid(x):
    return 1.0 / (1.0 + jnp.exp(-x))


def _rope_kernel(pos_ref, invf_ref, sign_ref, cos_ref, sin_ref):
    half = pos_ref.shape[0] // 2
    lo = lax.broadcasted_iota(jnp.int32, (half, LANES), 1) < HEAD_DIM // 2
    pos = jnp.where(lo, pos_ref[0:half, :], pos_ref[half:, :]).astype(F32)
    ang = pos * invf_ref[...]
    for fn, dst, sign in ((jnp.cos, cos_ref, None), (jnp.sin, sin_ref, sign_ref[...])):
        t = fn(ang)
        t_sw = pltpu.roll(t, HEAD_DIM // 2, 1)
        first, second = jnp.where(lo, t, t_sw), jnp.where(lo, t_sw, t)
        dst[0:half, :] = first if sign is None else first * sign
        dst[half:, :] = second if sign is None else second * sign


def _rope_tables(positions):
    m = positions.size
    half = HEAD_DIM // 2
    inv_freq = 1.0 / (ROPE_BASE ** (jnp.arange(0, half, dtype=F32) * (2.0 / HEAD_DIM)))
    invf = jnp.concatenate([inv_freq, inv_freq])[None, :]
    sign = jnp.concatenate([-jnp.ones((half,), F32), jnp.ones((half,), F32)])[None, :]
    pos = positions.reshape(m, 1)
    row = lambda i: (i, 0)
    const = lambda i: (0, 0)
    return pl.pallas_call(
        _rope_kernel,
        grid=(m // TM_ROPE,),
        in_specs=[pl.BlockSpec((TM_ROPE, 1), row),
                  pl.BlockSpec((1, LANES), const),
                  pl.BlockSpec((1, LANES), const)],
        out_specs=[pl.BlockSpec((TM_ROPE, LANES), row),
                   pl.BlockSpec((TM_ROPE, LANES), row)],
        out_shape=[jax.ShapeDtypeStruct((m, LANES), F32)] * 2,
        compiler_params=pltpu.CompilerParams(dimension_semantics=("arbitrary",)),
        name="rope_tables",
    )(pos, invf, sign)


def _head_block(ref, c, hd):
    return ref[c * RET_CHUNK:(c + 1) * RET_CHUNK, hd * HEAD_DIM:(hd + 1) * HEAD_DIM]


def _mixer_kernel(x_ref, cos_ref, sin_ref, g_ref, w_ref, cw_ref, cb_ref, lng_ref, lnb_ref,
                  wout_ref, mask_ref, xi_ref, zeta_ref, cd_ref, gng_ref, gnb_ref,
                  o_ref,
                  hbuf, state, h_buf, q_buf, k_buf, v_buf, gate_buf, conv_buf, ret_buf,
                  sc_buf, st_buf):
    tm = x_ref.shape[0]
    n_cb = CONV_CH // LANES
    n_chunks = tm // RET_CHUNK
    blocks = [(c, hd) for c in range(n_chunks) for hd in range(RET_HEADS)]

    @pl.when(pl.program_id(1) == 0)
    def _():
        hbuf[:, 0:2 * HALO_ROWS, :] = jnp.zeros((n_cb, 2 * HALO_ROWS, LANES), F32)
        state[...] = jnp.zeros(state.shape, F32)

    def conv_rows(rb):
        first = HALO_ROWS - (CONV_WIDTH - 1)
        r0 = rb * CONV_ROW_BLOCK
        accs = []
        for cb in range(n_cb):
            c0 = cb * LANES
            acc = None
            for t in range(CONV_WIDTH):
                seg = hbuf[cb, pl.ds(2 * (first + r0 + t), CONV_ROW_BLOCK, stride=2), :]
                term = seg * cw_ref[t:t + 1, c0:c0 + LANES]
                acc = term if acc is None else acc + term
            accs.append(acc + cb_ref[:, c0:c0 + LANES])
        tot = accs[0]
        for cb in range(1, n_cb):
            tot = tot + accs[cb]
        mu = jnp.sum(tot, axis=-1, keepdims=True) * (1.0 / CONV_CH)
        devs = [acc - mu for acc in accs]
        sq = devs[0] * devs[0]
        for cb in range(1, n_cb):
            sq = sq + devs[cb] * devs[cb]
        var = jnp.sum(sq, axis=-1, keepdims=True) * (1.0 / CONV_CH)
        inv = lax.rsqrt(var + LN_EPS)
        for cb in range(n_cb):
            c0 = cb * LANES
            y = devs[cb] * inv * lng_ref[:, c0:c0 + LANES] + lnb_ref[:, c0:c0 + LANES]
            conv_buf[r0:r0 + CONV_ROW_BLOCK, c0:c0 + LANES] = (y * _sigmoid(y)).astype(BF16)

    def rotary(z, scale):
        cos = cos_ref[...]
        sin = sin_ref[...]
        heads = []
        for hd in range(z.shape[1] // HEAD_DIM):
            zh = z[:, hd * HEAD_DIM:(hd + 1) * HEAD_DIM]
            zr = zh * cos + pltpu.roll(zh, HEAD_DIM // 2, 1) * sin
            heads.append(zr if scale is None else zr * scale)
        return jnp.concatenate(heads, axis=1)

    h_buf[...] = _rms_norm(x_ref[...], g_ref[...]).astype(BF16)
    for c0 in range(0, CONV_CH, PROJ_PIECE):
        a = _dot(h_buf[...], w_ref[:, c0:c0 + PROJ_PIECE])
        b = _dot(h_buf[...], w_ref[:, CONV_CH + c0:CONV_CH + c0 + PROJ_PIECE])
        glu = a * _sigmoid(b)
        for l0 in range(0, PROJ_PIECE, LANES):
            hbuf[(c0 + l0) // LANES, pl.ds(2 * HALO_ROWS, tm, stride=2), :] = glu[:, l0:l0 + LANES]

    q_col = 2 * CONV_CH
    finish = (lambda z: rotary(z, None).astype(BF16),
              lambda z: rotary(z, HEAD_DIM ** -0.5).astype(BF16),
              lambda z: z.astype(BF16),
              lambda z: z * _sigmoid(z))
    dsts = (q_buf, k_buf, v_buf, gate_buf)
    n_pieces = len(dsts) * (RET_WIDTH // PROJ_PIECE)
    n_rb = tm // CONV_ROW_BLOCK
    for i in range(n_pieces):
        j, c0 = divmod(i * PROJ_PIECE, RET_WIDTH)
        z = _dot(h_buf[...], w_ref[:, q_col + j * RET_WIDTH + c0:q_col + j * RET_WIDTH + c0 + PROJ_PIECE])
        dsts[j][:, c0:c0 + PROJ_PIECE] = finish[j](z)
        for rb in range(i * n_rb // n_pieces, (i + 1) * n_rb // n_pieces):
            conv_rows(rb)

    for i, (c, hd) in enumerate(blocks):
        qh, kh, vh = _head_block(q_buf, c, hd), _head_block(k_buf, c, hd), _head_block(v_buf, c, hd)
        s = lax.dot_general(qh, kh, (((1,), (1,)), ((), ())), preferred_element_type=F32)
        sc_buf[i] = (s * mask_ref[hd]).astype(BF16)
        kz = (kh.astype(F32) * zeta_ref[hd]).astype(BF16)
        kv = lax.dot_general(kz, vh, (((0,), (0,)), ((), ())), preferred_element_type=F32)
        st = state[hd]
        st_buf[i] = st.astype(BF16)
        state[hd] = st * cd_ref[hd] + kv

    for i, (c, hd) in enumerate(blocks):
        r0, c0 = c * RET_CHUNK, hd * HEAD_DIM
        qx = (_head_block(q_buf, c, hd).astype(F32) * xi_ref[hd]).astype(BF16)
        lhs = jnp.concatenate([sc_buf[i], qx], axis=1)
        rhs = jnp.concatenate([_head_block(v_buf, c, hd), st_buf[i]], axis=0)
        o = _dot(lhs, rhs)
        mu = jnp.mean(o, axis=-1, keepdims=True)
        d = o - mu
        var = jnp.mean(d * d, axis=-1, keepdims=True)
        y = d * lax.rsqrt(var + LN_EPS) * gng_ref[:, c0:c0 + HEAD_DIM] + gnb_ref[:, c0:c0 + HEAD_DIM]
        ret_buf[r0:r0 + RET_CHUNK, c0:c0 + HEAD_DIM] = (_head_block(gate_buf, c, hd) * y).astype(BF16)

    o_ref[...] = x_ref[...] + _dot(ret_buf[...], wout_ref[CONV_CH:, :])
    for r0 in range(0, tm, tm // 2):
        o_ref[r0:r0 + tm // 2, :] += _dot(conv_buf[r0:r0 + tm // 2, :], wout_ref[0:CONV_CH, :])

    for cb in range(n_cb):
        hbuf[cb, pl.ds(0, HALO_ROWS, stride=2), :] = hbuf[cb, pl.ds(2 * tm, HALO_ROWS, stride=2), :]


def _retention_constants():
    log_g = jnp.log(1.0 - jnp.exp2(-5.0 - jnp.arange(RET_HEADS, dtype=F32)))
    idx = jnp.arange(RET_CHUNK, dtype=F32)
    diff = idx[:, None] - idx[None, :]
    mask = jnp.where(diff[None] >= 0,
                     jnp.exp(jnp.maximum(diff, 0.0)[None] * log_g[:, None, None]), 0.0)
    zeta = jnp.exp((RET_CHUNK - 1 - idx)[None, :] * log_g[:, None])
    xi = jnp.exp((idx + 1.0)[None, :] * log_g[:, None])
    cd = jnp.exp(RET_CHUNK * log_g)
    full = (RET_HEADS, RET_CHUNK, HEAD_DIM)
    return (mask,
            jnp.broadcast_to(xi[:, :, None], full),
            jnp.broadcast_to(zeta[:, :, None], full),
            jnp.broadcast_to(cd[:, None, None], full))


def _mixer(x, cos_t, sin_t, norm_g, w_in, cw, cb, lng, lnb, w_out, consts, gng, gnb):
    bsz, seq, _ = x.shape
    tm = TM_MIXER
    n_blocks = (tm // RET_CHUNK) * RET_HEADS
    tile = lambda b, s: (b, s, 0)
    const2 = lambda b, s: (0, 0)
    const3 = lambda b, s: (0, 0, 0)
    resident = lambda shape: pl.BlockSpec(shape, const2, pipeline_mode=pl.Buffered(1))
    head_const = pl.BlockSpec((RET_HEADS, RET_CHUNK, HEAD_DIM), const3)
    return pl.pallas_call(
        _mixer_kernel,
        grid=(bsz, seq // tm),
        in_specs=[pl.BlockSpec((None, tm, D_MODEL), tile),
                  pl.BlockSpec((None, tm, LANES), tile),
                  pl.BlockSpec((None, tm, LANES), tile),
                  pl.BlockSpec((1, D_MODEL), const2),
                  resident(w_in.shape),
                  pl.BlockSpec((CONV_WIDTH, CONV_CH), const2),
                  pl.BlockSpec((1, CONV_CH), const2),
                  pl.BlockSpec((1, CONV_CH), const2),
                  pl.BlockSpec((1, CONV_CH), const2),
                  resident(w_out.shape),
                  head_const, head_const, head_const, head_const,
                  pl.BlockSpec((1, RET_WIDTH), const2),
                  pl.BlockSpec((1, RET_WIDTH), const2)],
        out_specs=pl.BlockSpec((None, tm, D_MODEL), tile),
        out_shape=jax.ShapeDtypeStruct(x.shape, F32),
        scratch_shapes=[
            pltpu.VMEM((CONV_CH // LANES, 2 * (HALO_ROWS + tm), LANES), F32),
            pltpu.VMEM((RET_HEADS, HEAD_DIM, HEAD_DIM), F32),
            pltpu.VMEM((tm, D_MODEL), BF16),
            pltpu.VMEM((tm, RET_WIDTH), BF16),
            pltpu.VMEM((tm, RET_WIDTH), BF16),
            pltpu.VMEM((tm, RET_WIDTH), BF16),
            pltpu.VMEM((tm, RET_WIDTH), F32),
            pltpu.VMEM((tm, CONV_CH), BF16),
            pltpu.VMEM((tm, RET_WIDTH), BF16),
            pltpu.VMEM((n_blocks, RET_CHUNK, RET_CHUNK), BF16),
            pltpu.VMEM((n_blocks, HEAD_DIM, HEAD_DIM), BF16),
        ],
        compiler_params=pltpu.CompilerParams(
            dimension_semantics=("arbitrary", "arbitrary"),
            vmem_limit_bytes=VMEM_LIMIT_BYTES),
        name="mixer",
    )(x, cos_t, sin_t, norm_g, w_in, cw, cb, lng, lnb, w_out, *consts, gng, gnb)


def _ffn_ple_kernel(x_ref, p_ref, gf_ref, w1_ref, w2_ref, gp_ref, wg_ref, wp_ref, gl_ref,
                    o_ref, *, final_norm):
    subs = [slice(r0, r0 + FFN_SUB) for r0 in range(0, x_ref.shape[0], FFN_SUB)]
    for rows in subs:
        x = x_ref[rows, :]
        h = _rms_norm(x, gf_ref[...]).astype(BF16)
        acc = None
        for j in range(D_FF // FF_CHUNK):
            c0 = j * FF_CHUNK
            a = jnp.maximum(_dot(h, w1_ref[:, c0:c0 + FF_CHUNK]), 0.0)
            part = _dot((a * a).astype(BF16), w2_ref[c0:c0 + FF_CHUNK, :])
            acc = part if acc is None else acc + part
        o_ref[rows, :] = x + acc
    for rows in subs:
        x = o_ref[rows, :]
        h = _rms_norm(x, gp_ref[...]).astype(BF16)
        gate = _sigmoid(_dot(h, wg_ref[...]))
        x = x + gate * _dot(p_ref[rows, :].astype(BF16), wp_ref[...])
        if final_norm:
            x = _rms_norm(x, gl_ref[...])
        o_ref[rows, :] = x


def _ffn_ple(x, p, layer, gf, w1, w2, gp, wg, wp, gl, final_norm):
    m = x.shape[0]
    tm = TM_FFN
    row = lambda i: (i, 0)
    const = lambda i: (0, 0)
    resident = lambda shape: pl.BlockSpec(shape, const, pipeline_mode=pl.Buffered(1))
    return pl.pallas_call(
        functools.partial(_ffn_ple_kernel, final_norm=final_norm),
        grid=(m // tm,),
        in_specs=[pl.BlockSpec((tm, D_MODEL), row),
                  pl.BlockSpec((None, tm, PLE_DIM), lambda i: (layer, i, 0)),
                  pl.BlockSpec((1, D_MODEL), const),
                  resident((D_MODEL, D_FF)),
                  resident((D_FF, D_MODEL)),
                  pl.BlockSpec((1, D_MODEL), const),
                  resident((D_MODEL, D_MODEL)),
                  resident((PLE_DIM, D_MODEL)),
                  pl.BlockSpec((1, D_MODEL), const)],
        out_specs=pl.BlockSpec((tm, D_MODEL), row),
        out_shape=jax.ShapeDtypeStruct(x.shape, F32),
        compiler_params=pltpu.CompilerParams(
            dimension_semantics=("arbitrary",),
            vmem_limit_bytes=VMEM_LIMIT_BYTES),
        name="ffn_ple_final" if final_norm else "ffn_ple",
    )(x, p, gf, w1, w2, gp, wg, wp, gl)


def kernel(x, p, positions, norm_mix_g, w_in, conv_dw_w, conv_dw_b, conv_ln_g, conv_ln_b,
           ret_gn_g, ret_gn_b, w_out, norm_ffn_g, w_ff1, w_ff2, norm_ple_g,
           w_ple_gate, w_ple_proj, final_norm_g):
    bsz, seq, d = x.shape
    depth = w_in.shape[0]
    assert d == D_MODEL and seq % TM_MIXER == 0
    assert (bsz * seq) % TM_FFN == 0 and (bsz * seq) % TM_ROPE == 0

    cos_t, sin_t = _rope_tables(positions)
    cos_t = cos_t.reshape(bsz, seq, LANES)
    sin_t = sin_t.reshape(bsz, seq, LANES)
    consts = _retention_constants()
    row = lambda v: v[None, :]

    for i in range(depth):
        x = _mixer(
            x, cos_t, sin_t, row(norm_mix_g[i]), w_in[i].astype(BF16),
            conv_dw_w[i], row(conv_dw_b[i]), row(conv_ln_g[i]), row(conv_ln_b[i]),
            w_out[i].astype(BF16), consts, row(ret_gn_g[i]), row(ret_gn_b[i]))
        x = _ffn_ple(
            x.reshape(bsz * seq, d), p.reshape(depth, bsz * seq, PLE_DIM), i,
            row(norm_ffn_g[i]), w_ff1[i].astype(BF16), w_ff2[i].astype(BF16),
            row(norm_ple_g[i]), w_ple_gate[i].astype(BF16), w_ple_proj[i].astype(BF16),
            row(final_norm_g), final_norm=(i == depth - 1)).reshape(bsz, seq, d)
    return x
```

```python
import functools

import jax
import jax.numpy as jnp
from jax import lax
from jax.experimental import pallas as pl
from jax.experimental.pallas import tpu as pltpu

D_MODEL = 1024
CONV_CH = 512
CONV_WIDTH = 31
RET_HEADS = 4
HEAD_DIM = 128
RET_WIDTH = RET_HEADS * HEAD_DIM
RET_CHUNK = 128
ROPE_BASE = 10000.0
D_FF = 4 * D_MODEL
PLE_DIM = 256
RMS_EPS = 1e-6
LN_EPS = 1e-5

LANES = 128
VMEM_LIMIT_BYTES = 56 * 1024 * 1024
MIXER_VMEM_LIMIT_BYTES = 60 * 1024 * 1024

HALO_ROWS = 32
CONV_ROW_BLOCK = 64
PROJ_PIECE = 256
TM_MIXER = 1024
TM_FFN = 1024
FFN_SUB = 512
FF_CHUNK = 1024
TM_ROPE = 2048

F32 = jnp.float32
BF16 = jnp.bfloat16


def _dot(a, b):
    return jnp.dot(a, b, preferred_element_type=F32)


def _rms_norm(x, g):
    ms = jnp.mean(x * x, axis=-1, keepdims=True)
    return x * lax.rsqrt(ms + RMS_EPS) * g


def _sigmoid(x):
    return 1.0 / (1.0 + jnp.exp(-x))


def _rope_kernel(pos_ref, invf_ref, sign_ref, cos_ref, sin_ref):
    half = pos_ref.shape[0] // 2
    lo = lax.broadcasted_iota(jnp.int32, (half, LANES), 1) < HEAD_DIM // 2
    pos = jnp.where(lo, pos_ref[0:half, :], pos_ref[half:, :]).astype(F32)
    ang = pos * invf_ref[...]
    for fn, dst, sign in ((jnp.cos, cos_ref, None), (jnp.sin, sin_ref, sign_ref[...])):
        t = fn(ang)
        t_sw = pltpu.roll(t, HEAD_DIM // 2, 1)
        first, second = jnp.where(lo, t, t_sw), jnp.where(lo, t_sw, t)
        dst[0:half, :] = first if sign is None else first * sign
        dst[half:, :] = second if sign is None else second * sign


def _rope_tables(positions):
    m = positions.size
    half = HEAD_DIM // 2
    inv_freq = 1.0 / (ROPE_BASE ** (jnp.arange(0, half, dtype=F32) * (2.0 / HEAD_DIM)))
    invf = jnp.concatenate([inv_freq, inv_freq])[None, :]
    sign = jnp.concatenate([-jnp.ones((half,), F32), jnp.ones((half,), F32)])[None, :]
    pos = positions.reshape(m, 1)
    row = lambda i: (i, 0)
    const = lambda i: (0, 0)
    return pl.pallas_call(
        _rope_kernel,
        grid=(m // TM_ROPE,),
        in_specs=[pl.BlockSpec((TM_ROPE, 1), row),
                  pl.BlockSpec((1, LANES), const),
                  pl.BlockSpec((1, LANES), const)],
        out_specs=[pl.BlockSpec((TM_ROPE, LANES), row),
                   pl.BlockSpec((TM_ROPE, LANES), row)],
        out_shape=[jax.ShapeDtypeStruct((m, LANES), F32)] * 2,
        compiler_params=pltpu.CompilerParams(dimension_semantics=("arbitrary",)),
        name="rope_tables",
    )(pos, invf, sign)


def _head_block(ref, c, hd):
    return ref[c * RET_CHUNK:(c + 1) * RET_CHUNK, hd * HEAD_DIM:(hd + 1) * HEAD_DIM]


def _mixer_kernel(x_ref, xn_ref, cos_ref, sin_ref, g_ref, w_ref, cw_ref, cb_ref, lng_ref, lnb_ref,
                  wout_ref, mask_ref, xi_ref, zeta_ref, cd_ref, gng_ref, gnb_ref,
                  o_ref,
                  hbuf, state, h2_buf, q_buf, k_buf, v_buf, gate_buf, conv_buf, ret_buf,
                  sc_buf, st_buf):
    tm = x_ref.shape[0]
    n_cb = CONV_CH // LANES
    n_chunks = tm // RET_CHUNK
    blocks = [(c, hd) for c in range(n_chunks) for hd in range(RET_HEADS)]

    @pl.when(pl.program_id(1) == 0)
    def _():
        hbuf[:, 0:2 * HALO_ROWS, :] = jnp.zeros((n_cb, 2 * HALO_ROWS, LANES), F32)
        state[...] = jnp.zeros(state.shape, F32)
        h2_buf[0] = _rms_norm(x_ref[...], g_ref[...]).astype(BF16)

    slot = pl.program_id(1) % 2

    def conv_rows(rb):
        first = HALO_ROWS - (CONV_WIDTH - 1)
        r0 = rb * CONV_ROW_BLOCK
        accs = []
        for cb in range(n_cb):
            c0 = cb * LANES
            acc = None
            for t in range(CONV_WIDTH):
                seg = hbuf[cb, pl.ds(2 * (first + r0 + t), CONV_ROW_BLOCK, stride=2), :]
                term = seg * cw_ref[t:t + 1, c0:c0 + LANES]
                acc = term if acc is None else acc + term
            accs.append(acc + cb_ref[:, c0:c0 + LANES])
        tot = accs[0]
        for cb in range(1, n_cb):
            tot = tot + accs[cb]
        mu = jnp.sum(tot, axis=-1, keepdims=True) * (1.0 / CONV_CH)
        devs = [acc - mu for acc in accs]
        sq = devs[0] * devs[0]
        for cb in range(1, n_cb):
            sq = sq + devs[cb] * devs[cb]
        var = jnp.sum(sq, axis=-1, keepdims=True) * (1.0 / CONV_CH)
        inv = lax.rsqrt(var + LN_EPS)
        for cb in range(n_cb):
            c0 = cb * LANES
            y = devs[cb] * inv * lng_ref[:, c0:c0 + LANES] + lnb_ref[:, c0:c0 + LANES]
            conv_buf[r0:r0 + CONV_ROW_BLOCK, c0:c0 + LANES] = (y * _sigmoid(y)).astype(BF16)

    def rotary(z, scale):
        cos = cos_ref[...]
        sin = sin_ref[...]
        heads = []
        for hd in range(z.shape[1] // HEAD_DIM):
            zh = z[:, hd * HEAD_DIM:(hd + 1) * HEAD_DIM]
            zr = zh * cos + pltpu.roll(zh, HEAD_DIM // 2, 1) * sin
            heads.append(zr if scale is None else zr * scale)
        return jnp.concatenate(heads, axis=1)

    for c0 in range(0, CONV_CH, PROJ_PIECE):
        a = _dot(h2_buf[slot], w_ref[:, c0:c0 + PROJ_PIECE])
        b = _dot(h2_buf[slot], w_ref[:, CONV_CH + c0:CONV_CH + c0 + PROJ_PIECE])
        glu = a * _sigmoid(b)
        for l0 in range(0, PROJ_PIECE, LANES):
            hbuf[(c0 + l0) // LANES, pl.ds(2 * HALO_ROWS, tm, stride=2), :] = glu[:, l0:l0 + LANES]

    q_col = 2 * CONV_CH
    finish = (lambda z: rotary(z, None).astype(BF16),
              lambda z: rotary(z, HEAD_DIM ** -0.5).astype(BF16),
              lambda z: z.astype(BF16),
              lambda z: z * _sigmoid(z))
    dsts = (q_buf, k_buf, v_buf, gate_buf)
    n_pieces = len(dsts) * (RET_WIDTH // PROJ_PIECE)
    n_rb = tm // CONV_ROW_BLOCK
    for i in range(n_pieces):
        j, c0 = divmod(i * PROJ_PIECE, RET_WIDTH)
        z = _dot(h2_buf[slot], w_ref[:, q_col + j * RET_WIDTH + c0:q_col + j * RET_WIDTH + c0 + PROJ_PIECE])
        dsts[j][:, c0:c0 + PROJ_PIECE] = finish[j](z)
        for rb in range(i * n_rb // n_pieces, (i + 1) * n_rb // n_pieces):
            conv_rows(rb)

    for i, (c, hd) in enumerate(blocks):
        qh, kh, vh = _head_block(q_buf, c, hd), _head_block(k_buf, c, hd), _head_block(v_buf, c, hd)
        s = lax.dot_general(qh, kh, (((1,), (1,)), ((), ())), preferred_element_type=F32)
        sc_buf[i] = (s * mask_ref[hd]).astype(BF16)
        kz = (kh.astype(F32) * zeta_ref[hd]).astype(BF16)
        kv = lax.dot_general(kz, vh, (((0,), (0,)), ((), ())), preferred_element_type=F32)
        st = state[hd]
        st_buf[i] = st.astype(BF16)
        state[hd] = st * cd_ref[hd] + kv

    for i, (c, hd) in enumerate(blocks):
        r0, c0 = c * RET_CHUNK, hd * HEAD_DIM
        qx = (_head_block(q_buf, c, hd).astype(F32) * xi_ref[hd]).astype(BF16)
        lhs = jnp.concatenate([sc_buf[i], qx], axis=1)
        rhs = jnp.concatenate([_head_block(v_buf, c, hd), st_buf[i]], axis=0)
        o = _dot(lhs, rhs)
        mu = jnp.mean(o, axis=-1, keepdims=True)
        d = o - mu
        var = jnp.mean(d * d, axis=-1, keepdims=True)
        y = d * lax.rsqrt(var + LN_EPS) * gng_ref[:, c0:c0 + HEAD_DIM] + gnb_ref[:, c0:c0 + HEAD_DIM]
        ret_buf[r0:r0 + RET_CHUNK, c0:c0 + HEAD_DIM] = (_head_block(gate_buf, c, hd) * y).astype(BF16)

    o_ref[...] = x_ref[...] + _dot(ret_buf[...], wout_ref[CONV_CH:, :])
    for r0 in range(0, tm, tm // 2):
        o_ref[r0:r0 + tm // 2, :] += _dot(conv_buf[r0:r0 + tm // 2, :], wout_ref[0:CONV_CH, :])

    h2_buf[1 - slot] = _rms_norm(xn_ref[...], g_ref[...]).astype(BF16)

    for cb in range(n_cb):
        hbuf[cb, pl.ds(0, HALO_ROWS, stride=2), :] = hbuf[cb, pl.ds(2 * tm, HALO_ROWS, stride=2), :]


def _retention_constants():
    log_g = jnp.log(1.0 - jnp.exp2(-5.0 - jnp.arange(RET_HEADS, dtype=F32)))
    idx = jnp.arange(RET_CHUNK, dtype=F32)
    diff = idx[:, None] - idx[None, :]
    mask = jnp.where(diff[None] >= 0,
                     jnp.exp(jnp.maximum(diff, 0.0)[None] * log_g[:, None, None]), 0.0)
    zeta = jnp.exp((RET_CHUNK - 1 - idx)[None, :] * log_g[:, None])
    xi = jnp.exp((idx + 1.0)[None, :] * log_g[:, None])
    cd = jnp.exp(RET_CHUNK * log_g)
    full = (RET_HEADS, RET_CHUNK, HEAD_DIM)
    return (mask,
            jnp.broadcast_to(xi[:, :, None], full),
            jnp.broadcast_to(zeta[:, :, None], full),
            jnp.broadcast_to(cd[:, None, None], full))


def _mixer(x, cos_t, sin_t, norm_g, w_in, cw, cb, lng, lnb, w_out, consts, gng, gnb):
    bsz, seq, _ = x.shape
    tm = TM_MIXER
    n_blocks = (tm // RET_CHUNK) * RET_HEADS
    tile = lambda b, s: (b, s, 0)
    next_tile = lambda b, s: (b, jnp.minimum(s + 1, seq // tm - 1), 0)
    const2 = lambda b, s: (0, 0)
    const3 = lambda b, s: (0, 0, 0)
    resident = lambda shape: pl.BlockSpec(shape, const2, pipeline_mode=pl.Buffered(1))
    head_const = pl.BlockSpec((RET_HEADS, RET_CHUNK, HEAD_DIM), const3)
    return pl.pallas_call(
        _mixer_kernel,
        grid=(bsz, seq // tm),
        in_specs=[pl.BlockSpec((None, tm, D_MODEL), tile),
                  pl.BlockSpec((None, tm, D_MODEL), next_tile),
                  pl.BlockSpec((None, tm, LANES), tile),
                  pl.BlockSpec((None, tm, LANES), tile),
                  pl.BlockSpec((1, D_MODEL), const2),
                  resident(w_in.shape),
                  pl.BlockSpec((CONV_WIDTH, CONV_CH), const2),
                  pl.BlockSpec((1, CONV_CH), const2),
                  pl.BlockSpec((1, CONV_CH), const2),
                  pl.BlockSpec((1, CONV_CH), const2),
                  resident(w_out.shape),
                  head_const, head_const, head_const, head_const,
                  pl.BlockSpec((1, RET_WIDTH), const2),
                  pl.BlockSpec((1, RET_WIDTH), const2)],
        out_specs=pl.BlockSpec((None, tm, D_MODEL), tile),
        out_shape=jax.ShapeDtypeStruct(x.shape, F32),
        scratch_shapes=[
            pltpu.VMEM((CONV_CH // LANES, 2 * (HALO_ROWS + tm), LANES), F32),
            pltpu.VMEM((RET_HEADS, HEAD_DIM, HEAD_DIM), F32),
            pltpu.VMEM((2, tm, D_MODEL), BF16),
            pltpu.VMEM((tm, RET_WIDTH), BF16),
            pltpu.VMEM((tm, RET_WIDTH), BF16),
            pltpu.VMEM((tm, RET_WIDTH), BF16),
            pltpu.VMEM((tm, RET_WIDTH), F32),
            pltpu.VMEM((tm, CONV_CH), BF16),
            pltpu.VMEM((tm, RET_WIDTH), BF16),
            pltpu.VMEM((n_blocks, RET_CHUNK, RET_CHUNK), BF16),
            pltpu.VMEM((n_blocks, HEAD_DIM, HEAD_DIM), BF16),
        ],
        compiler_params=pltpu.CompilerParams(
            dimension_semantics=("arbitrary", "arbitrary"),
            vmem_limit_bytes=MIXER_VMEM_LIMIT_BYTES),
        name="mixer",
    )(x, x, cos_t, sin_t, norm_g, w_in, cw, cb, lng, lnb, w_out, *consts, gng, gnb)


def _ffn_ple_kernel(x_ref, p_ref, gf_ref, w1_ref, w2_ref, gp_ref, wg_ref, wp_ref, gl_ref,
                    o_ref, *, final_norm):
    subs = [slice(r0, r0 + FFN_SUB) for r0 in range(0, x_ref.shape[0], FFN_SUB)]
    for rows in subs:
        x = x_ref[rows, :]
        h = _rms_norm(x, gf_ref[...]).astype(BF16)
        acc = None
        for j in range(D_FF // FF_CHUNK):
            c0 = j * FF_CHUNK
            a = jnp.maximum(_dot(h, w1_ref[:, c0:c0 + FF_CHUNK]), 0.0)
            part = _dot((a * a).astype(BF16), w2_ref[c0:c0 + FF_CHUNK, :])
            acc = part if acc is None else acc + part
        o_ref[rows, :] = x + acc
    for rows in subs:
        x = o_ref[rows, :]
        h = _rms_norm(x, gp_ref[...]).astype(BF16)
        gate = _sigmoid(_dot(h, wg_ref[...]))
        x = x + gate * _dot(p_ref[rows, :].astype(BF16), wp_ref[...])
        if final_norm:
            x = _rms_norm(x, gl_ref[...])
        o_ref[rows, :] = x


def _ffn_ple(x, p, layer, gf, w1, w2, gp, wg, wp, gl, final_norm):
    m = x.shape[0]
    tm = TM_FFN
    row = lambda i: (i, 0)
    const = lambda i: (0, 0)
    resident = lambda shape: pl.BlockSpec(shape, const, pipeline_mode=pl.Buffered(1))
    return pl.pallas_call(
        functools.partial(_ffn_ple_kernel, final_norm=final_norm),
        grid=(m // tm,),
        in_specs=[pl.BlockSpec((tm, D_MODEL), row),
                  pl.BlockSpec((None, tm, PLE_DIM), lambda i: (layer, i, 0)),
                  pl.BlockSpec((1, D_MODEL), const),
                  resident((D_MODEL, D_FF)),
                  resident((D_FF, D_MODEL)),
                  pl.BlockSpec((1, D_MODEL), const),
                  resident((D_MODEL, D_MODEL)),
                  resident((PLE_DIM, D_MODEL)),
                  pl.BlockSpec((1, D_MODEL), const)],
        out_specs=pl.BlockSpec((tm, D_MODEL), row),
        out_shape=jax.ShapeDtypeStruct(x.shape, F32),
        compiler_params=pltpu.CompilerParams(
            dimension_semantics=("arbitrary",),
            vmem_limit_bytes=VMEM_LIMIT_BYTES),
        name="ffn_ple_final" if final_norm else "ffn_ple",
    )(x, p, gf, w1, w2, gp, wg, wp, gl)


def kernel(x, p, positions, norm_mix_g, w_in, conv_dw_w, conv_dw_b, conv_ln_g, conv_ln_b,
           ret_gn_g, ret_gn_b, w_out, norm_ffn_g, w_ff1, w_ff2, norm_ple_g,
           w_ple_gate, w_ple_proj, final_norm_g):
    bsz, seq, d = x.shape
    depth = w_in.shape[0]
    assert d == D_MODEL and seq % TM_MIXER == 0
    assert (bsz * seq) % TM_FFN == 0 and (bsz * seq) % TM_ROPE == 0

    cos_t, sin_t = _rope_tables(positions)
    cos_t = cos_t.reshape(bsz, seq, LANES)
    sin_t = sin_t.reshape(bsz, seq, LANES)
    consts = _retention_constants()
    row = lambda v: v[None, :]

    for i in range(depth):
        x = _mixer(
            x, cos_t, sin_t, row(norm_mix_g[i]), w_in[i].astype(BF16),
            conv_dw_w[i], row(conv_dw_b[i]), row(conv_ln_g[i]), row(conv_ln_b[i]),
            w_out[i].astype(BF16), consts, row(ret_gn_g[i]), row(ret_gn_b[i]))
        x = _ffn_ple(
            x.reshape(bsz * seq, d), p.reshape(depth, bsz * seq, PLE_DIM), i,
            row(norm_ffn_g[i]), w_ff1[i].astype(BF16), w_ff2[i].astype(BF16),
            row(norm_ple_g[i]), w_ple_gate[i].astype(BF16), w_ple_proj[i].astype(BF16),
            row(final_norm_g), final_norm=(i == depth - 1)).reshape(bsz, seq, d)
    return x
```
